```python
import math
import jax, jax.numpy as jnp
from jax import lax
import numpy as np

D_MODEL = 4096
BATCH = 1
SEQ = 8192
DEPTH = 4
DEC_BATCH = 16
DEC_SEQ = 16
PAST_LEN = 2048

CHUNK = 64
N_MIXERS = 3
N_SSM_LAYERS = (DEPTH + 2) // N_MIXERS
N_FOX_LAYERS = (DEPTH + 1) // N_MIXERS
N_CONV_LAYERS = DEPTH // N_MIXERS
SSM_GROUP_CH = 16
SSM_GROUPS = D_MODEL // SSM_GROUP_CH
SSM_STATE = 64
DT_MIN = 1e-3
DT_MAX = 1e-1
FOX_HEAD_DIM = 128
FOX_HEADS = D_MODEL // FOX_HEAD_DIM
Q_BLOCK = 128
CONV_WIDTH = 3
MOE_GROUPS = 8
MOE_EXPERTS_PER_GROUP = 8
MOE_EXPERTS = MOE_GROUPS * MOE_EXPERTS_PER_GROUP
MOE_TOP_K = 2
MOE_D_EXPERT = D_MODEL // 8
MOE_BLOCK = 128
ALPHA = (2 * DEPTH) ** 0.25
BETA = (8 * DEPTH) ** -0.25
LN_EPS = 1e-5
ADA_SCALE = 0.3

kernel_name = 'hybrid_stream_s5_fox_shortconv_hmoe_step'


def layer_norm(x, g, b):
    xf = x.astype(jnp.float32)
    mu = jnp.mean(xf, axis=-1, keepdims=True)
    var = jnp.mean(jnp.square(xf - mu), axis=-1, keepdims=True)
    return ((xf - mu) * lax.rsqrt(var + LN_EPS) * g.astype(jnp.float32) + b.astype(jnp.float32)).astype(x.dtype)


def _complex_affine_combine(e1, e2):
    a1r, a1i, b1r, b1i = e1
    a2r, a2i, b2r, b2i = e2
    ar = a1r * a2r - a1i * a2i
    ai = a1r * a2i + a1i * a2r
    br = a2r * b1r - a2i * b1i + b2r
    bi = a2r * b1i + a2i * b1r + b2i
    return ar, ai, br, bi


def s5_mixer(u, h_re, h_im, lam_re, lam_im, log_dt, b_re, b_im, c_re, c_im, d_skip, w_glu, b_glu):
    f32 = jnp.float32
    bsz, t, _ = u.shape
    lam_re = lam_re.astype(f32)
    lam_im = lam_im.astype(f32)
    dt = jnp.exp(log_dt.astype(f32))[:, None]
    mag = jnp.exp(lam_re * dt)
    ab_re = mag * jnp.cos(lam_im * dt)
    ab_im = mag * jnp.sin(lam_im * dt)
    den = lam_re * lam_re + lam_im * lam_im
    nr = ab_re - 1.0
    z_re = (nr * lam_re + ab_im * lam_im) / den
    z_im = (ab_im * lam_re - nr * lam_im) / den
    b_re = b_re.astype(f32)
    b_im = b_im.astype(f32)
    bb_re = z_re[..., None] * b_re - z_im[..., None] * b_im
    bb_im = z_re[..., None] * b_im + z_im[..., None] * b_re
    c_re = c_re.astype(f32)
    c_im = c_im.astype(f32)
    blk = CHUNK if t % CHUNK == 0 else t
    nblk = t // blk
    ug = u.astype(f32).reshape(bsz, nblk, blk, SSM_GROUPS, SSM_GROUP_CH).swapaxes(0, 1)

    def chunk_step(carry, u_blk):
        hr, hi = carry
        bu_re = jnp.einsum('btgc,gpc->btgp', u_blk, bb_re)
        bu_im = jnp.einsum('btgc,gpc->btgp', u_blk, bb_im)
        bu_re = bu_re.at[:, 0].add(ab_re * hr - ab_im * hi)
        bu_im = bu_im.at[:, 0].add(ab_re * hi + ab_im * hr)
        a_re = jnp.broadcast_to(ab_re, bu_re.shape)
        a_im = jnp.broadcast_to(ab_im, bu_im.shape)
        _, _, s_re, s_im = lax.associative_scan(_complex_affine_combine, (a_re, a_im, bu_re, bu_im), axis=1)
        y = jnp.einsum('btgp,gcp->btgc', s_re, c_re) - jnp.einsum('btgp,gcp->btgc', s_im, c_im)
        return (s_re[:, -1], s_im[:, -1]), y

    (hr, hi), ys = lax.scan(chunk_step, (h_re.astype(f32), h_im.astype(f32)), ug)
    y = ys.swapaxes(0, 1).reshape(bsz, t, D_MODEL).astype(u.dtype) + d_skip * u
    z = jax.nn.gelu(y) @ w_glu + b_glu
    za, zb = jnp.split(z, 2, axis=-1)
    return za * jax.nn.sigmoid(zb), hr, hi


def fox_attention(q, k, v, cum_q, cum_k, q_start):
    bsz, t, nh, hd = q.shape
    qb = Q_BLOCK if t % Q_BLOCK == 0 else t
    nb = t // qb
    k_pos = jnp.arange(k.shape[1])
    ck = cum_k.swapaxes(1, 2)
    scale = hd ** -0.5

    def block(args):
        qi, cqi, bi = args
        q_pos = q_start + bi * qb + jnp.arange(qb)
        s = jnp.einsum('bqhd,bkhd->bhqk', qi, k).astype(jnp.float32) * scale
        s = s + cqi.swapaxes(1, 2)[..., None] - ck[:, :, None, :]
        s = jnp.where(k_pos[None, :] <= q_pos[:, None], s, -jnp.inf)
        p = jax.nn.softmax(s, axis=-1)
        return jnp.einsum('bhqk,bkhd->bqhd', p.astype(v.dtype), v)

    qs = q.reshape(bsz, nb, qb, nh, hd).swapaxes(0, 1)
    cs = cum_q.reshape(bsz, nb, qb, nh).swapaxes(0, 1)
    o = lax.map(block, (qs, cs, jnp.arange(nb)))
    return o.swapaxes(0, 1).reshape(bsz, t, nh * hd)


def fox_mixer(u, cache, w_qkvf, b_f, w_o):
    bsz, t, d = u.shape
    proj = u @ w_qkvf
    q = proj[..., :d].reshape(bsz, t, FOX_HEADS, FOX_HEAD_DIM)
    k = proj[..., d:2 * d].reshape(bsz, t, FOX_HEADS, FOX_HEAD_DIM)
    v = proj[..., 2 * d:3 * d].reshape(bsz, t, FOX_HEADS, FOX_HEAD_DIM)
    lf = jax.nn.log_sigmoid((proj[..., 3 * d:] + b_f).astype(jnp.float32))
    if cache is None:
        k_all, v_all, lf_all, start = k, v, lf, 0
    else:
        k_c, v_c, lf_c = cache
        k_all = jnp.concatenate([k_c.astype(k.dtype), k], axis=1)
        v_all = jnp.concatenate([v_c.astype(v.dtype), v], axis=1)
        lf_all = jnp.concatenate([lf_c.astype(jnp.float32), lf], axis=1)
        start = k_c.shape[1]
    cum = jnp.cumsum(lf_all, axis=1)
    o = fox_attention(q, k_all, v_all, cum[:, -t:], cum, start)
    return o @ w_o, k, v, lf


def short_conv_mixer(u, buf, w_in, conv_w, w_out):
    t = u.shape[1]
    gate_b, gate_c, hin = jnp.split(u @ w_in, 3, axis=-1)
    z = gate_c * hin
    zp = jnp.concatenate([buf.astype(z.dtype), z], axis=1)
    acc = conv_w[0] * zp[:, 0:t]
    for j in range(1, CONV_WIDTH):
        acc = acc + conv_w[j] * zp[:, j:j + t]
    return (gate_b * acc) @ w_out, zp[:, -(CONV_WIDTH - 1):]


def routed_experts(xf, expert_id, gate, w_gate, w_up, w_down):
    n, d = xf.shape
    k = expert_id.shape[1]
    a = n * k
    flat_e = expert_id.reshape(a)
    order = jnp.argsort(flat_e)
    sorted_e = flat_e[order]
    counts = jnp.zeros((MOE_EXPERTS,), jnp.int32).at[flat_e].add(1)
    padded = (counts + MOE_BLOCK - 1) // MOE_BLOCK * MOE_BLOCK
    pad_end = jnp.cumsum(padded)
    pad_start = pad_end - padded
    start = jnp.cumsum(counts) - counts
    slot = pad_start[sorted_e] + jnp.arange(a, dtype=jnp.int32) - start[sorted_e]
    n_blocks = -(-a // MOE_BLOCK) + MOE_EXPERTS
    tok_of_slot = jnp.full((n_blocks * MOE_BLOCK,), n, jnp.int32).at[slot].set((order // k).astype(jnp.int32))
    block_expert = jnp.minimum(
        jnp.searchsorted(pad_end, jnp.arange(n_blocks, dtype=jnp.int32) * MOE_BLOCK, side='right'),
        MOE_EXPERTS - 1).astype(jnp.int32)
    x_pad = jnp.concatenate([xf, jnp.zeros((1, d), xf.dtype)], axis=0)

    def run_block(args):
        toks, e = args
        xb = x_pad[toks]
        hb = jax.nn.silu(xb @ w_gate[e]) * (xb @ w_up[e])
        return hb @ w_down[e]

    y_slots = lax.map(run_block, (tok_of_slot.reshape(n_blocks, MOE_BLOCK), block_expert))
    slot_of_assign = jnp.zeros((a,), jnp.int32).at[order].set(slot)
    y = y_slots.reshape(-1, d)[slot_of_assign].reshape(n, k, d)
    return jnp.einsum('nkd,nk->nd', y, gate.astype(y.dtype))


def hier_moe(h, w_group, b_group, w_expert, b_expert, w_gate, w_up, w_down):
    bsz, t, d = h.shape
    xf = h.reshape(bsz * t, d)
    n = xf.shape[0]
    g_logits = (xf @ w_group + b_group).astype(jnp.float32)
    g_prob = jax.nn.softmax(g_logits, axis=-1)
    _, g_idx = lax.top_k(g_logits, 1)
    g_w = jnp.take_along_axis(g_prob, g_idx, axis=-1)
    e_logits = (xf @ w_expert + b_expert).astype(jnp.float32).reshape(n, MOE_GROUPS, MOE_EXPERTS_PER_GROUP)
    e_in = jnp.take_along_axis(e_logits, g_idx[:, :, None], axis=1)[:, 0]
    e_prob = jax.nn.softmax(e_in, axis=-1)
    top_p, top_j = lax.top_k(e_prob, MOE_TOP_K)
    gate = g_w * top_p / jnp.sum(top_p, axis=-1, keepdims=True)
    expert_id = (g_idx * MOE_EXPERTS_PER_GROUP + top_j).astype(jnp.int32)
    return routed_experts(xf, expert_id, gate, w_gate, w_up, w_down).reshape(bsz, t, d)


def setup_inputs(seed: int = 0) -> dict:
    key = jax.random.key(seed)
    keys = iter(jax.random.split(key, 64))
    f32 = jnp.float32

    def nrm(shape, scale=1.0):
        return jax.random.normal(next(keys), shape, f32) * scale

    def uni(shape, lo, hi):
        return jax.random.uniform(next(keys), shape, f32, lo, hi)

    D = D_MODEL
    G, P, GC, H, HD = SSM_GROUPS, SSM_STATE, SSM_GROUP_CH, FOX_HEADS, FOX_HEAD_DIM
    nA, nB, nC = N_SSM_LAYERS, N_FOX_LAYERS, N_CONV_LAYERS
    E, DE = MOE_EXPERTS, MOE_D_EXPERT
    x_prompt = nrm((BATCH, SEQ, D))
    x_sample = nrm((DEC_BATCH, DEC_SEQ, D))
    c_prompt = nrm((BATCH, D))
    c_sample = nrm((DEC_BATCH, D))
    state_ssm_re = nrm((nA, DEC_BATCH, G, P), 0.5)
    state_ssm_im = nrm((nA, DEC_BATCH, G, P), 0.5)
    cache_fox_k = nrm((nB, DEC_BATCH, PAST_LEN, H, HD))
    cache_fox_v = nrm((nB, DEC_BATCH, PAST_LEN, H, HD))
    cache_fox_logf = jax.nn.log_sigmoid(2.5 + nrm((nB, DEC_BATCH, PAST_LEN, H)))
    state_conv = nrm((nC, DEC_BATCH, CONV_WIDTH - 1, D))
    ssm_lam_re = -0.5 + nrm((nA, G, P), 0.01)
    ssm_lam_im = jnp.pi * jnp.arange(P, dtype=f32) + nrm((nA, G, P), 0.01)
    ssm_log_dt = uni((nA, G), math.log(DT_MIN), math.log(DT_MAX))
    ssm_b_re = nrm((nA, G, P, GC), (2 * GC) ** -0.5)
    ssm_b_im = nrm((nA, G, P, GC), (2 * GC) ** -0.5)
    ssm_c_re = nrm((nA, G, GC, P), P ** -0.5)
    ssm_c_im = nrm((nA, G, GC, P), P ** -0.5)
    ssm_d = nrm((nA, D), 0.5)
    ssm_w_glu = nrm((nA, D, 2 * D), BETA * D ** -0.5)
    ssm_b_glu = nrm((nA, 2 * D), 0.01)
    fox_w_qkvf = jnp.concatenate([
        nrm((nB, D, D), D ** -0.5),
        nrm((nB, D, D), D ** -0.5),
        nrm((nB, D, D), BETA * D ** -0.5),
        nrm((nB, D, H), D ** -0.5),
    ], axis=-1)
    fox_b_f = uni((nB, H), 1.0, 4.0)
    fox_w_o = nrm((nB, D, D), BETA * D ** -0.5)
    conv_w_in = nrm((nC, D, 3 * D), D ** -0.5)
    conv_w = nrm((nC, CONV_WIDTH, D), CONV_WIDTH ** -0.5)
    conv_w_out = nrm((nC, D, D), BETA * D ** -0.5)
    ada_w = nrm((DEPTH, D, 6 * D), ADA_SCALE * D ** -0.5)
    ada_b = nrm((DEPTH, 6 * D), 0.01)
    ln_g = 1.0 + nrm((DEPTH, 2, D), 0.01)
    ln_b = nrm((DEPTH, 2, D), 0.01)
    moe_w_group = nrm((DEPTH, D, MOE_GROUPS), D ** -0.5)
    moe_b_group = nrm((DEPTH, MOE_GROUPS), 0.01)
    moe_w_expert = nrm((DEPTH, D, E), D ** -0.5)
    moe_b_expert = nrm((DEPTH, E), 0.01)
    moe_w_gate = nrm((DEPTH, E, D, DE), D ** -0.5)
    moe_w_up = nrm((DEPTH, E, D, DE), D ** -0.5)
    moe_w_down = nrm((DEPTH, E, DE, D), BETA * DE ** -0.5)
    return {
        'x_prompt': x_prompt, 'x_sample': x_sample, 'c_prompt': c_prompt, 'c_sample': c_sample,
        'state_ssm_re': state_ssm_re, 'state_ssm_im': state_ssm_im,
        'cache_fox_k': cache_fox_k, 'cache_fox_v': cache_fox_v, 'cache_fox_logf': cache_fox_logf,
        'state_conv': state_conv,
        'ssm_lam_re': ssm_lam_re, 'ssm_lam_im': ssm_lam_im, 'ssm_log_dt': ssm_log_dt,
        'ssm_b_re': ssm_b_re, 'ssm_b_im': ssm_b_im, 'ssm_c_re': ssm_c_re, 'ssm_c_im': ssm_c_im,
        'ssm_d': ssm_d, 'ssm_w_glu': ssm_w_glu, 'ssm_b_glu': ssm_b_glu,
        'fox_w_qkvf': fox_w_qkvf, 'fox_b_f': fox_b_f, 'fox_w_o': fox_w_o,
        'conv_w_in': conv_w_in, 'conv_w': conv_w, 'conv_w_out': conv_w_out,
        'ada_w': ada_w, 'ada_b': ada_b, 'ln_g': ln_g, 'ln_b': ln_b,
        'moe_w_group': moe_w_group, 'moe_b_group': moe_b_group,
        'moe_w_expert': moe_w_expert, 'moe_b_expert': moe_b_expert,
        'moe_w_gate': moe_w_gate, 'moe_w_up': moe_w_up, 'moe_w_down': moe_w_down,
    }


def reference(x_prompt, x_sample, c_prompt, c_sample, state_ssm_re, state_ssm_im,
              cache_fox_k, cache_fox_v, cache_fox_logf, state_conv,
              ssm_lam_re, ssm_lam_im, ssm_log_dt, ssm_b_re, ssm_b_im, ssm_c_re, ssm_c_im,
              ssm_d, ssm_w_glu, ssm_b_glu, fox_w_qkvf, fox_b_f, fox_w_o,
              conv_w_in, conv_w, conv_w_out, ada_w, ada_b, ln_g, ln_b,
              moe_w_group, moe_b_group, moe_w_expert, moe_b_expert,
              moe_w_gate, moe_w_up, moe_w_down):

    def trunk(x, c, ssm_re, ssm_im, fox_cache, conv_buf):
        c_act = jax.nn.silu(c)
        re_out, im_out, k_out, v_out, lf_out, conv_out = [], [], [], [], [], []
        for i in range(DEPTH):
            mod = (c_act @ ada_w[i] + ada_b[i])[:, None, :]
            sh1, sc1, g1, sh2, sc2, g2 = jnp.split(mod, 6, axis=-1)
            h = x * (1.0 + sc1) + sh1
            j = i // N_MIXERS
            kind = i % N_MIXERS
            if kind == 0:
                out, hr, hi = s5_mixer(h, ssm_re[j], ssm_im[j], ssm_lam_re[j], ssm_lam_im[j], ssm_log_dt[j],
                                       ssm_b_re[j], ssm_b_im[j], ssm_c_re[j], ssm_c_im[j], ssm_d[j],
                                       ssm_w_glu[j], ssm_b_glu[j])
                re_out.append(hr)
                im_out.append(hi)
            elif kind == 1:
                layer_cache = None if fox_cache is None else (fox_cache[0][j], fox_cache[1][j], fox_cache[2][j])
                out, k_new, v_new, lf_new = fox_mixer(h, layer_cache, fox_w_qkvf[j], fox_b_f[j], fox_w_o[j])
                k_out.append(k_new)
                v_out.append(v_new)
                lf_out.append(lf_new)
            else:
                out, buf = short_conv_mixer(h, conv_buf[j], conv_w_in[j], conv_w[j], conv_w_out[j])
                conv_out.append(buf)
            x = layer_norm(ALPHA * x + (1.0 + g1) * out, ln_g[i, 0], ln_b[i, 0])
            h = x * (1.0 + sc2) + sh2
            ffn = hier_moe(h, moe_w_group[i], moe_b_group[i], moe_w_expert[i], moe_b_expert[i],
                           moe_w_gate[i], moe_w_up[i], moe_w_down[i])
            x = layer_norm(ALPHA * x + (1.0 + g2) * ffn, ln_g[i, 1], ln_b[i, 1])
        return x, jnp.stack(re_out), jnp.stack(im_out), jnp.stack(k_out), jnp.stack(v_out), jnp.stack(lf_out), jnp.stack(conv_out)

    bp = x_prompt.shape[0]
    zero_ssm = jnp.zeros((N_SSM_LAYERS, bp, SSM_GROUPS, SSM_STATE), jnp.float32)
    zero_conv = jnp.zeros((N_CONV_LAYERS, bp, CONV_WIDTH - 1, D_MODEL), x_prompt.dtype)
    y_prompt, p_ssm_re, p_ssm_im, p_fox_k, p_fox_v, p_fox_logf, p_conv = trunk(
        x_prompt, c_prompt, zero_ssm, zero_ssm, None, zero_conv)
    y_sample, s_ssm_re, s_ssm_im, s_fox_k, s_fox_v, s_fox_logf, s_conv = trunk(
        x_sample, c_sample, state_ssm_re, state_ssm_im, (cache_fox_k, cache_fox_v, cache_fox_logf), state_conv)
    return (y_prompt, y_sample, p_ssm_re, p_ssm_im, p_fox_k, p_fox_v, p_fox_logf, p_conv,
            s_ssm_re, s_ssm_im, s_fox_k, s_fox_v, s_fox_logf, s_conv)
```

```python
import functools

import jax
import jax.numpy as jnp
from jax import lax
from jax.experimental import pallas as pl
from jax.experimental.pallas import tpu as pltpu

LN_EPS = 1e-5
N_MIXERS = 3

V7X_VMEM_BYTES = 64 * 1024 * 1024
VMEM_LIMIT_BYTES = V7X_VMEM_BYTES - 8 * 1024 * 1024
LANES = 128
SUBLANES = 8

F32 = jnp.float32
BF16 = jnp.bfloat16


def _pick(n, candidates):
    for c in candidates:
        if c <= n and n % c == 0:
            return c
    return n


def _cparams(sem):
    return pltpu.CompilerParams(dimension_semantics=sem, vmem_limit_bytes=VMEM_LIMIT_BYTES)


def _drop_aliased(kernel, n_in, n_alias):
    def wrapped(*refs):
        kernel(*(refs[:n_in] + refs[n_in + n_alias:]))
    return wrapped


def _with_prev(args, in_specs, prev):
    aliases = {}
    n_in = len(args)
    if prev is not None:
        for k, p in enumerate(prev):
            args.append(p)
            in_specs.append(pl.BlockSpec(memory_space=pl.ANY))
            aliases[n_in + k] = k
    return n_in, aliases


def _mm_kernel(*refs, nk, glu, has_bias, epilogue, lhs_silu):
    it = iter(refs)
    x_ref = next(it)
    w_ref = next(it)
    w2_ref = next(it) if glu else None
    b_ref = next(it) if has_bias else None
    b2_ref = next(it) if (has_bias and glu) else None
    o_ref = next(it)
    acc_ref = next(it)
    acc2_ref = next(it) if glu else None
    k = pl.program_id(2)

    x = x_ref[...]
    if lhs_silu:
        xf = x.astype(F32)
        x = xf * jax.nn.sigmoid(xf)
    xb = x.astype(BF16)
    part = jnp.dot(xb, w_ref[...].astype(BF16), preferred_element_type=F32)
    part2 = jnp.dot(xb, w2_ref[...].astype(BF16), preferred_element_type=F32) if glu else None

    @pl.when(k == 0)
    def _():
        acc_ref[...] = part
        if glu:
            acc2_ref[...] = part2

    @pl.when(k > 0)
    def _():
        acc_ref[...] += part
        if glu:
            acc2_ref[...] += part2

    @pl.when(k == nk - 1)
    def _():
        z = acc_ref[...]
        if has_bias:
            z = z + b_ref[...]
        if glu:
            z2 = acc2_ref[...]
            if has_bias:
                z2 = z2 + b2_ref[...]
            z = z * jax.nn.sigmoid(z2)
        if epilogue == "logsigmoid":
            z = jax.nn.log_sigmoid(z)
        o_ref[...] = z.astype(o_ref.dtype)


def _matmul(x, w, layer, *, n_out, col_off=0, bias=None, glu=False, epilogue=None,
            lhs_silu=False, out_dtype=F32, name="mm"):
    m, kdim = x.shape
    nw = w.shape[-1]
    if m <= 64:
        tm = m
        tn = _pick(n_out, (2048, 1024, 512, 256, 128))
        tk = _pick(kdim, (1024, 512, 256, 128))
    else:
        tm = _pick(m, (1408, 1024, 768, 512, 256, 128, 64, 32, 16, 8))
        tn = _pick(n_out, (512, 256, 128) if glu else (1024, 512, 256, 128))
        tk = _pick(kdim, (512, 256, 128))
    assert col_off % tn == 0 and n_out % tn == 0
    nk = kdim // tk
    coff = col_off // tn
    goff = (col_off + n_out) // tn
    in_specs = [pl.BlockSpec((tm, tk), lambda i, j, k: (i, k)),
                pl.BlockSpec((None, tk, tn), lambda i, j, k: (layer, k, j + coff))]
    args = [x, w]
    if glu:
        in_specs.append(pl.BlockSpec((None, tk, tn), lambda i, j, k: (layer, k, j + goff)))
        args.append(w)
    if bias is not None:
        b3 = bias.reshape(bias.shape[0], 1, nw)
        in_specs.append(pl.BlockSpec((None, 1, tn), lambda i, j, k: (layer, 0, j + coff)))
        args.append(b3)
        if glu:
            in_specs.append(pl.BlockSpec((None, 1, tn), lambda i, j, k: (layer, 0, j + goff)))
            args.append(b3)
    scratch = [pltpu.VMEM((tm, tn), F32)]
    if glu:
        scratch.append(pltpu.VMEM((tm, tn), F32))
    kern = functools.partial(_mm_kernel, nk=nk, glu=glu, has_bias=bias is not None,
                             epilogue=epilogue, lhs_silu=lhs_silu)
    return pl.pallas_call(
        kern,
        out_shape=jax.ShapeDtypeStruct((m, n_out), out_dtype),
        grid=(m // tm, n_out // tn, nk),
        in_specs=in_specs,
        out_specs=pl.BlockSpec((tm, tn), lambda i, j, k: (i, j)),
        scratch_shapes=scratch,
        compiler_params=_cparams(("parallel", "parallel", "arbitrary")),
        name=name,
    )(*args)


def _expand_mod(ref, nb, rpm):
    v = ref[...]
    d = v.shape[-1]
    if nb == 1:
        return v.reshape(1, d)
    return jnp.broadcast_to(v, (nb, rpm, d)).reshape(nb * rpm, d)


def _route(logits):
    tm = logits.shape[0]
    lane = lax.broadcasted_iota(jnp.int32, (tm, LANES), 1)
    neg = jnp.float32(-jnp.inf)
    big = jnp.int32(LANES)
    glog = jnp.where(lane < 8, logits, neg)
    gmax = jnp.max(glog, axis=1, keepdims=True)
    gidx = jnp.min(jnp.where(glog == gmax, lane, big), axis=1, keepdims=True)
    gsum = jnp.sum(jnp.exp(glog - gmax), axis=1, keepdims=True)
    g_w = 1.0 / gsum
    emask = (lane >= 8) & (lane < 72) & (((lane - 8) >> 3) == gidx)
    elog = jnp.where(emask, logits, neg)
    emax = jnp.max(elog, axis=1, keepdims=True)
    i1 = jnp.min(jnp.where(elog == emax, lane, big), axis=1, keepdims=True)
    elog2 = jnp.where(lane == i1, neg, elog)
    emax2 = jnp.max(elog2, axis=1, keepdims=True)
    i2 = jnp.min(jnp.where(elog2 == emax2, lane, big), axis=1, keepdims=True)
    esum = jnp.sum(jnp.exp(elog - emax), axis=1, keepdims=True)
    p1 = 1.0 / esum
    p2 = jnp.exp(emax2 - emax) / esum
    psum = p1 + p2
    gate0 = g_w * p1 / psum
    gate1 = g_w * p2 / psum
    return jnp.where(lane == 0, (i1 - 8).astype(F32),
                     jnp.where(lane == 1, (i2 - 8).astype(F32),
                               jnp.where(lane == 2, gate0, jnp.where(lane == 3, gate1, 0.0))))


def _ln_kernel(*refs, alpha, nb, rpm, first, combine, emit_h, router):
    it = iter(refs)
    x_ref = next(it)
    if not first:
        o_ref = next(it)
        if combine:
            o2_ref = next(it)
            rt_in_ref = next(it)
        g_ref = next(it)
        lg_ref = next(it)
        lb_ref = next(it)
    if emit_h:
        sc_ref = next(it)
        sh_ref = next(it)
    if router:
        whi_ref = next(it)
        wlo_ref = next(it)
        br_ref = next(it)
    xo_ref = None if first else next(it)
    h_ref = next(it) if emit_h else None
    rt_ref = next(it) if router else None

    x = x_ref[...].astype(F32)
    if not first:
        o = o_ref[...].astype(F32)
        if combine:
            rt = rt_in_ref[...]
            o = rt[:, 2:3] * o + rt[:, 3:4] * o2_ref[...].astype(F32)
        g = _expand_mod(g_ref, nb, rpm)
        v = alpha * x + (1.0 + g) * o
        mu = jnp.mean(v, axis=-1, keepdims=True)
        vc = v - mu
        var = jnp.mean(vc * vc, axis=-1, keepdims=True)
        x = vc * lax.rsqrt(var + LN_EPS) * lg_ref[...].reshape(1, -1) + lb_ref[...].reshape(1, -1)
        xo_ref[...] = x
    if emit_h:
        sc = _expand_mod(sc_ref, nb, rpm)
        sh = _expand_mod(sh_ref, nb, rpm)
        h = x * (1.0 + sc) + sh
        h_ref[...] = h.astype(h_ref.dtype)
        if router:
            h_hi = h.astype(BF16)
            h_lo = (h - h_hi.astype(F32)).astype(BF16)
            whi = whi_ref[...]
            logits = (jnp.dot(h_hi, whi, preferred_element_type=F32)
                      + jnp.dot(h_lo, whi, preferred_element_type=F32)
                      + jnp.dot(h_hi, wlo_ref[...], preferred_element_type=F32)
                      + br_ref[...])
            rt_ref[...] = _route(logits)


def _ln_call(x, mods, *, row_off, rows, rpm, seq0, alpha=1.0, o=None, o2=None, route_in=None,
             res_mod=None, ln_g=None, ln_b=None, ln_idx=None, h_mods=None,
             h_dtype=None, router_w=None, prev=None, name="ln"):
    rtot, d = x.shape
    first = o is None
    combine = o2 is not None
    emit_h = h_dtype is not None
    router = router_w is not None
    tm = _pick(rows, (128, 64, 32, 16, 8))
    if tm > rpm:
        assert tm % rpm == 0 and rpm % SUBLANES == 0
        nb = tm // rpm
        assert seq0 % nb == 0
    else:
        assert rpm % tm == 0
        nb = 1
    assert row_off % tm == 0 and rows % tm == 0
    boff = row_off // tm

    def row_map(i):
        return (i + boff, 0)

    def mod_spec(lyr, chunk):
        if nb == 1:
            return pl.BlockSpec((None, 1, 1, d), lambda i: (lyr, seq0 + (i * tm) // rpm, 0, chunk))
        return pl.BlockSpec((None, nb, 1, d), lambda i: (lyr, seq0 // nb + i, 0, chunk))

    row_spec = pl.BlockSpec((tm, d), row_map)
    in_specs = [row_spec]
    args = [x]
    if not first:
        in_specs.append(row_spec)
        args.append(o)
        if combine:
            in_specs += [row_spec, pl.BlockSpec((tm, LANES), row_map)]
            args += [o2, route_in]
        in_specs.append(mod_spec(*res_mod))
        args.append(mods)
        in_specs += [pl.BlockSpec((None, None, 1, d), lambda i: (ln_idx[0], ln_idx[1], 0, 0))] * 2
        args += [ln_g, ln_b]
    if emit_h:
        in_specs += [mod_spec(h_mods[0], h_mods[1]), mod_spec(h_mods[0], h_mods[2])]
        args += [mods, mods]
    if router:
        whi, wlo, br = router_w
        in_specs += [pl.BlockSpec((d, LANES), lambda i: (0, 0)),
                     pl.BlockSpec((d, LANES), lambda i: (0, 0)),
                     pl.BlockSpec((1, LANES), lambda i: (0, 0))]
        args += [whi, wlo, br]
    out_shape = []
    out_specs = []
    if not first:
        out_shape.append(jax.ShapeDtypeStruct((rtot, d), F32))
        out_specs.append(row_spec)
    if emit_h:
        out_shape.append(jax.ShapeDtypeStruct((rtot, d), h_dtype))
        out_specs.append(row_spec)
    if router:
        out_shape.append(jax.ShapeDtypeStruct((rtot, LANES), F32))
        out_specs.append(pl.BlockSpec((tm, LANES), row_map))
    n_in, aliases = _with_prev(args, in_specs, prev)
    kern = functools.partial(_ln_kernel, alpha=alpha, nb=nb, rpm=rpm, first=first,
                             combine=combine, emit_h=emit_h, router=router)
    outs = pl.pallas_call(
        _drop_aliased(kern, n_in, len(aliases)),
        out_shape=out_shape,
        grid=(rows // tm,),
        in_specs=in_specs,
        out_specs=out_specs,
        input_output_aliases=aliases,
        compiler_params=_cparams(("parallel",)),
        name=name,
    )(*args)
    return list(outs)


def _ln_both(x, mods, geom, **kw):
    outs = _ln_call(x, mods, row_off=0, rows=geom["rp"], rpm=geom["seq"], seq0=geom["nbs"],
                    name="ln_prompt", **kw)
    return _ln_call(x, mods, row_off=geom["rp"], rows=geom["rs"], rpm=geom["dseq"], seq0=0,
                    prev=outs, name="ln_sample", **kw)


S5_GC = 16
S5_P = 64
S5_GPB = LANES // S5_GC
S5_CB = S5_GPB * S5_P // LANES


def _s5_kernel(u_ref, win_ref, wout_ref, are_ref, aim_ref, d_ref, h0r_ref, h0i_ref,
               y_ref, hr_ref, hi_ref, bre, bim, sre, sim, *, tl, pitch, nkb):
    t = pl.program_id(2)
    ncb = nkb * S5_CB
    half = S5_CB * LANES

    @pl.when(t == 0)
    def _():
        sre[...] = h0r_ref[...]
        sim[...] = h0i_ref[...]

    u = u_ref[...]
    ub = u.astype(BF16)
    for kk in range(nkb):
        bu = jnp.dot(ub[:, kk * LANES:(kk + 1) * LANES], win_ref[kk], preferred_element_type=F32)
        for j in range(S5_CB):
            c = kk * S5_CB + j
            bre[pl.ds(c * pitch, tl), :] = bu[:, j * LANES:(j + 1) * LANES]
            bim[pl.ds(c * pitch, tl), :] = bu[:, half + j * LANES:half + (j + 1) * LANES]

    nog = ncb // SUBLANES
    a_r = [are_ref[pl.ds(o * SUBLANES, SUBLANES), :] for o in range(nog)]
    a_i = [aim_ref[pl.ds(o * SUBLANES, SUBLANES), :] for o in range(nog)]
    s0 = tuple(sre[pl.ds(o * SUBLANES, SUBLANES), :] for o in range(nog)) + \
        tuple(sim[pl.ds(o * SUBLANES, SUBLANES), :] for o in range(nog))

    def step(tt, carry):
        new_r, new_i = [], []
        for o in range(nog):
            s_r, s_i = carry[o], carry[nog + o]
            idx = pl.ds(o * SUBLANES * pitch + tt, SUBLANES, stride=pitch)
            n_r = a_r[o] * s_r - a_i[o] * s_i + bre[idx, :]
            n_i = a_r[o] * s_i + a_i[o] * s_r + bim[idx, :]
            bre[idx, :] = n_r
            bim[idx, :] = n_i
            new_r.append(n_r)
            new_i.append(n_i)
        return tuple(new_r) + tuple(new_i)

    fin = lax.fori_loop(0, tl, step, s0)
    for o in range(nog):
        sre[pl.ds(o * SUBLANES, SUBLANES), :] = fin[o]
        sim[pl.ds(o * SUBLANES, SUBLANES), :] = fin[nog + o]
    hr_ref[...] = sre[...]
    hi_ref[...] = sim[...]

    ys = []
    for kk in range(nkb):
        parts = [bre[pl.ds((kk * S5_CB + j) * pitch, tl), :] for j in range(S5_CB)]
        parts += [bim[pl.ds((kk * S5_CB + j) * pitch, tl), :] for j in range(S5_CB)]
        lhs = jnp.concatenate(parts, axis=1).astype(BF16)
        ys.append(jnp.dot(lhs, wout_ref[kk], preferred_element_type=F32))
    y = jnp.concatenate(ys, axis=1) if nkb > 1 else ys[0]
    yy = y + d_ref[...] * u
    y_ref[...] = jax.nn.gelu(yy).astype(y_ref.dtype)


def _s5_weights(lam_re, lam_im, log_dt, b_re, b_im, c_re, c_im):
    g, p, gc = b_re.shape
    assert gc == S5_GC and p == S5_P
    dt = jnp.exp(log_dt.astype(F32))[:, None]
    mag = jnp.exp(lam_re * dt)
    ab_re = mag * jnp.cos(lam_im * dt)
    ab_im = mag * jnp.sin(lam_im * dt)
    den = lam_re * lam_re + lam_im * lam_im
    nr = ab_re - 1.0
    z_re = (nr * lam_re + ab_im * lam_im) / den
    z_im = (ab_im * lam_re - nr * lam_im) / den
    bb_re = z_re[..., None] * b_re - z_im[..., None] * b_im
    bb_im = z_re[..., None] * b_im + z_im[..., None] * b_re
    nkb = g // S5_GPB
    eye = jnp.eye(S5_GPB, dtype=F32)

    def in_blocks(bb):
        t = bb.reshape(nkb, S5_GPB, p, gc).transpose(0, 1, 3, 2)
        w = t[:, :, :, None, :] * eye[None, :, None, :, None]
        return w.reshape(nkb, S5_GPB * gc, S5_GPB * p)

    def out_blocks(cc):
        t = cc.reshape(nkb, S5_GPB, gc, p).transpose(0, 1, 3, 2)
        w = t[:, :, :, None, :] * eye[None, :, None, :, None]
        return w.reshape(nkb, S5_GPB * p, S5_GPB * gc)

    w_in = jnp.concatenate([in_blocks(bb_re), in_blocks(bb_im)], axis=2).astype(BF16)
    w_out = jnp.concatenate([out_blocks(c_re.astype(F32)), -out_blocks(c_im.astype(F32))], axis=1).astype(BF16)
    ncols = g * p // LANES
    return w_in, w_out, ab_re.reshape(ncols, LANES), ab_im.reshape(ncols, LANES)


def _s5_call(h, wts, d_skip, h0_re, h0_im, *, row_off, nseq, t_len, prev=None, name="s5"):
    rtot, d = h.shape
    w_in, w_out, a_re, a_im = wts
    ck = _pick(d, (1024, 512, 256, 128))
    nkb = ck // LANES
    ncb = nkb * S5_CB
    assert ncb % SUBLANES == 0
    nkg = d // ck
    tl = _pick(t_len, (256, 128, 64, 32, 16, 8))
    pitch = tl + SUBLANES if (tl // SUBLANES) % 2 == 0 else tl + 2 * SUBLANES
    nt = t_len // tl
    assert row_off % tl == 0
    boff = row_off // tl
    kern = functools.partial(_s5_kernel, tl=tl, pitch=pitch, nkb=nkb)
    in_specs = [
        pl.BlockSpec((tl, ck), lambda kg, b, t: (boff + b * nt + t, kg)),
        pl.BlockSpec((nkb, LANES, 2 * S5_CB * LANES), lambda kg, b, t: (kg, 0, 0)),
        pl.BlockSpec((nkb, 2 * S5_CB * LANES, LANES), lambda kg, b, t: (kg, 0, 0)),
        pl.BlockSpec((ncb, LANES), lambda kg, b, t: (kg, 0)),
        pl.BlockSpec((ncb, LANES), lambda kg, b, t: (kg, 0)),
        pl.BlockSpec((1, ck), lambda kg, b, t: (0, kg)),
        pl.BlockSpec((None, ncb, LANES), lambda kg, b, t: (b, kg, 0)),
        pl.BlockSpec((None, ncb, LANES), lambda kg, b, t: (b, kg, 0)),
    ]
    args = [h, w_in, w_out, a_re, a_im, d_skip, h0_re, h0_im]
    n_in, aliases = _with_prev(args, in_specs, prev)
    nstate = a_re.shape[0]
    out_shape = [jax.ShapeDtypeStruct((rtot, d), BF16),
                 jax.ShapeDtypeStruct((nseq, nstate, LANES), F32),
                 jax.ShapeDtypeStruct((nseq, nstate, LANES), F32)]
    out_specs = [pl.BlockSpec((tl, ck), lambda kg, b, t: (boff + b * nt + t, kg)),
                 pl.BlockSpec((None, ncb, LANES), lambda kg, b, t: (b, kg, 0)),
                 pl.BlockSpec((None, ncb, LANES), lambda kg, b, t: (b, kg, 0))]
    return pl.pallas_call(
        _drop_aliased(kern, n_in, len(aliases)),
        out_shape=out_shape,
        grid=(nkg, nseq, nt),
        in_specs=in_specs,
        out_specs=out_specs,
        scratch_shapes=[pltpu.VMEM((ncb * pitch, LANES), F32), pltpu.VMEM((ncb * pitch, LANES), F32),
                        pltpu.VMEM((ncb, LANES), F32), pltpu.VMEM((ncb, LANES), F32)],
        input_output_aliases=aliases,
        compiler_params=_cparams(("arbitrary", "arbitrary", "arbitrary")),
        name=name,
    )(*args)


def _cumsum_kernel(x_ref, o_ref, carry, *, tl):
    t = pl.program_id(1)

    @pl.when(t == 0)
    def _():
        carry[...] = jnp.zeros_like(carry)

    x = x_ref[...]
    row = lax.broadcasted_iota(jnp.int32, (tl, tl), 0)
    col = lax.broadcasted_iota(jnp.int32, (tl, tl), 1)
    tri = (col <= row).astype(BF16)
    x_hi = x.astype(BF16)
    r1 = x - x_hi.astype(F32)
    x_mid = r1.astype(BF16)
    x_lo = (r1 - x_mid.astype(F32)).astype(BF16)
    c = (jnp.dot(tri, x_hi, preferred_element_type=F32)
         + jnp.dot(tri, x_mid, preferred_element_type=F32)
         + jnp.dot(tri, x_lo, preferred_element_type=F32)) + carry[...]
    o_ref[...] = c
    carry[...] = c[tl - 1:tl, :]


def _cumsum_time(x):
    b, t, hh = x.shape
    tl = _pick(t, (512, 256, 128)) if t % 128 == 0 else t
    return pl.pallas_call(
        functools.partial(_cumsum_kernel, tl=tl),
        out_shape=jax.ShapeDtypeStruct((b, t, hh), F32),
        grid=(b, t // tl),
        in_specs=[pl.BlockSpec((None, tl, hh), lambda i, j: (i, j, 0))],
        out_specs=pl.BlockSpec((None, tl, hh), lambda i, j: (i, j, 0)),
        scratch_shapes=[pltpu.VMEM((1, hh), F32)],
        compiler_params=_cparams(("arbitrary", "arbitrary")),
        name="cumsum_time",
    )(x)


def _flash_kernel(qt_ref, kt_ref, q_ref, k_ref, v_ref, ck_ref, o_ref, m_s, l_s, acc_s, *, tq, scale):
    p = pl.program_id(1)
    qi = qt_ref[p]
    kj = kt_ref[p]

    @pl.when(kj == 0)
    def _():
        m_s[...] = jnp.full_like(m_s, -jnp.inf)
        l_s[...] = jnp.zeros_like(l_s)
        acc_s[...] = jnp.zeros_like(acc_s)

    q = q_ref[...].astype(BF16)
    k = k_ref[...].astype(BF16)
    s = lax.dot_general(q, k, (((1,), (1,)), ((), ())), preferred_element_type=F32) * scale
    s = s - ck_ref[...]
    row = lax.broadcasted_iota(jnp.int32, (tq, tq), 0) + qi * tq
    col = lax.broadcasted_iota(jnp.int32, (tq, tq), 1) + kj * tq
    s = jnp.where(col <= row, s, -jnp.inf)
    m_prev = m_s[...]
    m_new = jnp.maximum(m_prev, jnp.max(s, axis=1, keepdims=True))
    a = jnp.exp(m_prev - m_new)
    pexp = jnp.exp(s - m_new)
    l_s[...] = a * l_s[...] + jnp.sum(pexp, axis=1, keepdims=True)
    acc_s[...] = a * acc_s[...] + jnp.dot(pexp.astype(BF16), v_ref[...].astype(BF16),
                                          preferred_element_type=F32)
    m_s[...] = m_new

    @pl.when(kj == qi)
    def _():
        o_ref[...] = (acc_s[...] / l_s[...]).astype(o_ref.dtype)


def _flash_prefill(q, k, v, cum_t, *, t_len, heads, hd):
    rtot, d = q.shape
    tq = _pick(t_len, (1024, 512, 256, 128))
    nq = t_len // tq
    pairs = [(i, j) for i in range(nq) for j in range(i + 1)]
    qt = jnp.asarray([a for a, _ in pairs], jnp.int32)
    kt = jnp.asarray([b for _, b in pairs], jnp.int32)
    grid_spec = pltpu.PrefetchScalarGridSpec(
        num_scalar_prefetch=2,
        grid=(heads, len(pairs)),
        in_specs=[pl.BlockSpec((tq, hd), lambda h, p, qt, kt: (qt[p], h)),
                  pl.BlockSpec((tq, hd), lambda h, p, qt, kt: (kt[p], h)),
                  pl.BlockSpec((tq, hd), lambda h, p, qt, kt: (kt[p], h)),
                  pl.BlockSpec((None, 1, tq), lambda h, p, qt, kt: (h, 0, kt[p]))],
        out_specs=pl.BlockSpec((tq, hd), lambda h, p, qt, kt: (qt[p], h)),
        scratch_shapes=[pltpu.VMEM((tq, 1), F32), pltpu.VMEM((tq, 1), F32), pltpu.VMEM((tq, hd), F32)],
    )
    return pl.pallas_call(
        functools.partial(_flash_kernel, tq=tq, scale=float(hd) ** -0.5),
        out_shape=jax.ShapeDtypeStruct((rtot, d), BF16),
        grid_spec=grid_spec,
        compiler_params=_cparams(("arbitrary", "arbitrary")),
        name="fox_prefill",
    )(qt, kt, q, k, v, cum_t)


def _decode_kernel(q_ref, kn_ref, vn_ref, kc_ref, vc_ref, ck_ref, o_ref, *, past, s_len, scale):
    q = q_ref[...].astype(BF16)
    nt = (((1,), (1,)), ((), ()))
    ck = ck_ref[...]
    s_c = lax.dot_general(q, kc_ref[...].astype(BF16), nt, preferred_element_type=F32) * scale - ck[:, :past]
    s_n = lax.dot_general(q, kn_ref[...].astype(BF16), nt, preferred_element_type=F32) * scale - ck[:, past:]
    row = lax.broadcasted_iota(jnp.int32, (s_len, s_len), 0)
    col = lax.broadcasted_iota(jnp.int32, (s_len, s_len), 1)
    s_n = jnp.where(col <= row, s_n, -jnp.inf)
    m = jnp.maximum(jnp.max(s_c, axis=1, keepdims=True), jnp.max(s_n, axis=1, keepdims=True))
    p_c = jnp.exp(s_c - m)
    p_n = jnp.exp(s_n - m)
    l = jnp.sum(p_c, axis=1, keepdims=True) + jnp.sum(p_n, axis=1, keepdims=True)
    acc = (jnp.dot(p_c.astype(BF16), vc_ref[...].astype(BF16), preferred_element_type=F32)
           + jnp.dot(p_n.astype(BF16), vn_ref[...].astype(BF16), preferred_element_type=F32))
    o_ref[...] = (acc / l).astype(o_ref.dtype)


def _fox_decode(q, k, v, k_cache, v_cache, cum_t, *, row_off, nseq, s_len, heads, hd, layer, prev):
    rtot, d = q.shape
    past = k_cache.shape[2]
    assert row_off % s_len == 0 and past % LANES == 0
    boff = row_off // s_len
    new_spec = pl.BlockSpec((s_len, hd), lambda b, h: (boff + b, h))
    cache_spec = pl.BlockSpec((None, None, past, hd), lambda b, h: (layer, b, 0, h))
    in_specs = [new_spec, new_spec, new_spec, cache_spec, cache_spec,
                pl.BlockSpec((None, None, 1, past + s_len), lambda b, h: (b, h, 0, 0))]
    args = [q, k, v, k_cache, v_cache, cum_t]
    n_in, aliases = _with_prev(args, in_specs, prev)
    kern = functools.partial(_decode_kernel, past=past, s_len=s_len, scale=float(hd) ** -0.5)
    return pl.pallas_call(
        _drop_aliased(kern, n_in, len(aliases)),
        out_shape=[jax.ShapeDtypeStruct((rtot, d), BF16)],
        grid=(nseq, heads),
        in_specs=in_specs,
        out_specs=[new_spec],
        input_output_aliases=aliases,
        compiler_params=_cparams(("parallel", "parallel")),
        name="fox_decode",
    )(*args)[0]


def _conv_kernel(gb_ref, gc_ref, hin_ref, cw_ref, buf_ref, o_ref, zl_ref, carry, *, tl):
    t = pl.program_id(2)

    @pl.when(t == 0)
    def _():
        carry[...] = buf_ref[...]

    z = gc_ref[...] * hin_ref[...]
    c0 = carry[0:1, :]
    c1 = carry[1:2, :]
    row = lax.broadcasted_iota(jnp.int32, z.shape, 0)
    zm1 = jnp.where(row == 0, c1, pltpu.roll(z, 1, axis=0))
    zm2 = jnp.where(row == 0, c0, jnp.where(row == 1, c1, pltpu.roll(z, 2, axis=0)))
    cw = cw_ref[...]
    acc = cw[0:1, :] * zm2 + cw[1:2, :] * zm1 + cw[2:3, :] * z
    o_ref[...] = (gb_ref[...] * acc).astype(o_ref.dtype)
    last = z[tl - 2:tl, :]
    carry[...] = last
    zl_ref[...] = last


def _conv_call(proj, conv_w, layer, buf, *, row_off, nseq, t_len, d, prev=None, name="conv"):
    rtot = proj.shape[0]
    assert conv_w.shape[1] == 3 and t_len >= 2
    tn = _pick(d, (1024, 512, 256, 128))
    tl = _pick(t_len, (256, 128, 64, 32, 16, 8))
    nt = t_len // tl
    nd = d // tn
    assert row_off % tl == 0
    boff = row_off // tl

    def pspec(part):
        return pl.BlockSpec((tl, tn), lambda b, j, t: (boff + b * nt + t, part * nd + j))

    in_specs = [pspec(0), pspec(1), pspec(2),
                pl.BlockSpec((None, 3, tn), lambda b, j, t: (layer, 0, j)),
                pl.BlockSpec((None, 2, tn), lambda b, j, t: (b, 0, j))]
    args = [proj, proj, proj, conv_w, buf]
    n_in, aliases = _with_prev(args, in_specs, prev)
    return pl.pallas_call(
        _drop_aliased(functools.partial(_conv_kernel, tl=tl), n_in, len(aliases)),
        out_shape=[jax.ShapeDtypeStruct((rtot, d), BF16), jax.ShapeDtypeStruct((nseq, 2, d), F32)],
        grid=(nseq, nd, nt),
        in_specs=in_specs,
        out_specs=[pl.BlockSpec((tl, tn), lambda b, j, t: (boff + b * nt + t, j)),
                   pl.BlockSpec((None, 2, tn), lambda b, j, t: (b, 0, j))],
        scratch_shapes=[pltpu.VMEM((2, tn), F32)],
        input_output_aliases=aliases,
        compiler_params=_cparams(("arbitrary", "arbitrary", "arbitrary")),
        name=name,
    )(*args)


def _gather_kernel(idx_ref, nv_ref, src_ref, o_ref, sem, *, tg):
    i = pl.program_id(0)
    base = i * tg

    def row_copy(r, row):
        return pltpu.make_async_copy(src_ref.at[pl.ds(row, 1)], o_ref.at[pl.ds(r, 1)], sem)

    def start(r, c):
        row_copy(r, idx_ref[base + r]).start()
        return c

    def wait(r, c):
        row_copy(r, 0).wait()
        return c

    @pl.when(i < nv_ref[0])
    def _():
        lax.fori_loop(0, tg, start, 0)
        lax.fori_loop(0, tg, wait, 0)


def _gather_rows(src, idx, n_valid_rows=None, name="gather_rows"):
    m = idx.shape[0]
    d = src.shape[1]
    tg = _pick(m, (256, 128, 64, 32, 16, 8))
    if n_valid_rows is None:
        nv = jnp.full((1,), m // tg, jnp.int32)
    else:
        nv = (n_valid_rows // tg).astype(jnp.int32).reshape(1)
    grid_spec = pltpu.PrefetchScalarGridSpec(
        num_scalar_prefetch=2,
        grid=(m // tg,),
        in_specs=[pl.BlockSpec(memory_space=pl.ANY)],
        out_specs=pl.BlockSpec((tg, d), lambda i, idx, nv: (jnp.minimum(i, nv[0] - 1), 0)),
        scratch_shapes=[pltpu.SemaphoreType.DMA],
    )
    return pl.pallas_call(
        functools.partial(_gather_kernel, tg=tg),
        out_shape=jax.ShapeDtypeStruct((m, d), src.dtype),
        grid_spec=grid_spec,
        compiler_params=_cparams(("arbitrary",)),
        name=name,
    )(idx, nv, src)


def _expert_kernel(be_ref, nu_ref, x_ref, wg_ref, wu_ref, wd_ref, o_ref, hg, hu, hb, *, nkt):
    b = pl.program_id(0)
    j = pl.program_id(1)

    @pl.when(b < nu_ref[0])
    def _():
        @pl.when(j < nkt)
        def _():
            xb = x_ref[...].astype(BF16)
            pg = jnp.dot(xb, wg_ref[...].astype(BF16), preferred_element_type=F32)
            pu = jnp.dot(xb, wu_ref[...].astype(BF16), preferred_element_type=F32)

            @pl.when(j == 0)
            def _():
                hg[...] = pg
                hu[...] = pu

            @pl.when(j > 0)
            def _():
                hg[...] += pg
                hu[...] += pu

            @pl.when(j == nkt - 1)
            def _():
                g = hg[...]
                hb[...] = (g * jax.nn.sigmoid(g) * hu[...]).astype(BF16)

        @pl.when(j >= nkt)
        def _():
            o_ref[...] = jnp.dot(hb[...], wd_ref[...].astype(BF16), preferred_element_type=F32)


def _expert_call(xs, block_expert, n_used, w_gate, w_up, w_down, layer, *, tm):
    rows, d = xs.shape
    de = w_gate.shape[-1]
    nb = rows // tm
    tk = _pick(d, (1024, 512, 256, 128))
    tn = tk
    nkt = d // tk
    nnt = d // tn
    last_n = nnt - 1

    def blk(b, nu):
        return jnp.minimum(b, nu[0] - 1)

    def kidx(j):
        return jnp.minimum(j, nkt - 1)

    def nidx(b, j, nu):
        return jnp.where(b < nu[0], jnp.maximum(j - nkt, 0), last_n)

    grid_spec = pltpu.PrefetchScalarGridSpec(
        num_scalar_prefetch=2,
        grid=(nb, nkt + nnt),
        in_specs=[
            pl.BlockSpec((tm, tk), lambda b, j, be, nu: (blk(b, nu), kidx(j))),
            pl.BlockSpec((None, None, tk, de), lambda b, j, be, nu: (layer, be[blk(b, nu)], kidx(j), 0)),
            pl.BlockSpec((None, None, tk, de), lambda b, j, be, nu: (layer, be[blk(b, nu)], kidx(j), 0)),
            pl.BlockSpec((None, None, de, tn), lambda b, j, be, nu: (layer, be[blk(b, nu)], 0, nidx(b, j, nu))),
        ],
        out_specs=pl.BlockSpec((tm, tn), lambda b, j, be, nu: (blk(b, nu), nidx(b, j, nu))),
        scratch_shapes=[pltpu.VMEM((tm, de), F32), pltpu.VMEM((tm, de), F32), pltpu.VMEM((tm, de), BF16)],
    )
    return pl.pallas_call(
        functools.partial(_expert_kernel, nkt=nkt),
        out_shape=jax.ShapeDtypeStruct((rows, d), F32),
        grid_spec=grid_spec,
        compiler_params=_cparams(("arbitrary", "arbitrary")),
        name="moe_experts",
    )(block_expert, n_used, xs, w_gate, w_up, w_down)


def _dispatch(ids, n_experts, tm):
    r, k = ids.shape
    a = r * k
    flat_e = ids.reshape(a)
    order = jnp.argsort(flat_e, stable=True).astype(jnp.int32)
    sorted_e = flat_e[order]
    counts = jnp.zeros((n_experts,), jnp.int32).at[flat_e].add(1)
    padded = (counts + tm - 1) // tm * tm
    pad_end = jnp.cumsum(padded)
    pad_start = pad_end - padded
    start = jnp.cumsum(counts) - counts
    slot = pad_start[sorted_e] + jnp.arange(a, dtype=jnp.int32) - start[sorted_e]
    nb = -(-a // tm) + n_experts
    tok_of_slot = jnp.zeros((nb * tm,), jnp.int32).at[slot].set(order // k)
    block_expert = jnp.minimum(
        jnp.searchsorted(pad_end, jnp.arange(nb, dtype=jnp.int32) * tm, side="right"),
        n_experts - 1).astype(jnp.int32)
    n_used = (pad_end[-1] // tm).astype(jnp.int32).reshape(1)
    slot_of_assign = jnp.zeros((a,), jnp.int32).at[order].set(slot).reshape(r, k)
    return tok_of_slot, block_expert, n_used, slot_of_assign


def _moe(h, route, w_gate, w_up, w_down, layer):
    n_experts = w_gate.shape[1]
    ids = route[:, 0:2].astype(jnp.int32)
    a = ids.size
    tm = _pick(a, (512, 256, 128, 64, 32, 16, 8))
    tok_of_slot, block_expert, n_used, slot_of_assign = _dispatch(ids, n_experts, tm)
    xs = _gather_rows(h, tok_of_slot, n_valid_rows=n_used * tm, name="moe_gather_x")
    ys = _expert_call(xs, block_expert, n_used, w_gate, w_up, w_down, layer, tm=tm)
    y0 = _gather_rows(ys, slot_of_assign[:, 0], name="moe_gather_y0")
    y1 = _gather_rows(ys, slot_of_assign[:, 1], name="moe_gather_y1")
    return y0, y1


def kernel(x_prompt, x_sample, c_prompt, c_sample, state_ssm_re, state_ssm_im, cache_fox_k, cache_fox_v, cache_fox_logf, state_conv, ssm_lam_re, ssm_lam_im, ssm_log_dt, ssm_b_re, ssm_b_im, ssm_c_re, ssm_c_im, ssm_d, ssm_w_glu, ssm_b_glu, fox_w_qkvf, fox_b_f, fox_w_o, conv_w_in, conv_w, conv_w_out, ada_w, ada_b, ln_g, ln_b, moe_w_group, moe_b_group, moe_w_expert, moe_b_expert, moe_w_gate, moe_w_up, moe_w_down):
    bp, seq, d = x_prompt.shape
    nbs, dseq, _ = x_sample.shape
    depth = ada_w.shape[0]
    heads, hd = cache_fox_k.shape[3], cache_fox_k.shape[4]
    past = cache_fox_k.shape[2]
    n_groups = moe_w_group.shape[-1]
    n_experts = moe_w_expert.shape[-1]
    assert bp == 1 and n_groups == 8 and n_experts == 64 and heads * hd == d
    alpha = float((2 * depth) ** 0.25)
    rp, rs = bp * seq, nbs * dseq
    geom = dict(rp=rp, rs=rs, seq=seq, dseq=dseq, nbs=nbs)

    x = jnp.concatenate([x_prompt.reshape(rp, d), x_sample.reshape(rs, d)], axis=0)

    nmod = -(-(nbs + bp) // SUBLANES) * SUBLANES
    c_all = jnp.concatenate([c_sample, c_prompt, jnp.zeros((nmod - nbs - bp, d), F32)], axis=0)
    mods = jnp.stack([_matmul(c_all, ada_w, i, n_out=6 * d, bias=ada_b, lhs_silu=True, name="ada")
                      for i in range(depth)]).reshape(depth, nmod, 1, 6 * d)
    ln_g4 = ln_g.reshape(depth, 2, 1, d)
    ln_b4 = ln_b.reshape(depth, 2, 1, d)
    SH1, SC1, G1, SH2, SC2, G2 = range(6)

    def mixer_h_dtype(i):
        return F32 if i % N_MIXERS == 0 else BF16

    def router_weights(i):
        w = jnp.concatenate([moe_w_group[i], moe_w_expert[i],
                             jnp.zeros((d, LANES - n_groups - n_experts), F32)], axis=1)
        b = jnp.concatenate([moe_b_group[i], moe_b_expert[i],
                             jnp.zeros((LANES - n_groups - n_experts,), F32)]).reshape(1, LANES)
        w_hi = w.astype(BF16)
        w_lo = (w - w_hi.astype(F32)).astype(BF16)
        return w_hi, w_lo, b

    (h,) = _ln_both(x, mods, geom, h_mods=(0, SC1, SH1), h_dtype=mixer_h_dtype(0))

    re_p, im_p, re_s, im_s = [], [], [], []
    k_p, v_p, lf_p, k_s, v_s, lf_s = [], [], [], [], [], []
    conv_p, conv_s = [], []
    for i in range(depth):
        j = i // N_MIXERS
        kind = i % N_MIXERS
        if kind == 0:
            wts = _s5_weights(ssm_lam_re[j].astype(F32), ssm_lam_im[j].astype(F32), ssm_log_dt[j],
                              ssm_b_re[j].astype(F32), ssm_b_im[j].astype(F32), ssm_c_re[j], ssm_c_im[j])
            nstate = wts[2].shape[0]
            zero_state = jnp.zeros((bp, nstate, LANES), F32)
            d_skip = ssm_d[j].reshape(1, d)
            outs = _s5_call(h, wts, d_skip, zero_state, zero_state, row_off=0, nseq=bp, t_len=seq,
                            name="s5_prompt")
            outs2 = _s5_call(h, wts, d_skip, state_ssm_re[j].reshape(nbs, nstate, LANES),
                             state_ssm_im[j].reshape(nbs, nstate, LANES), row_off=rp, nseq=nbs,
                             t_len=dseq, prev=[outs[0]], name="s5_sample")
            gshape = ssm_lam_re.shape[1:]
            re_p.append(outs[1].reshape((bp,) + gshape))
            im_p.append(outs[2].reshape((bp,) + gshape))
            re_s.append(outs2[1].reshape((nbs,) + gshape))
            im_s.append(outs2[2].reshape((nbs,) + gshape))
            out = _matmul(outs2[0], ssm_w_glu, j, n_out=d, bias=ssm_b_glu, glu=True, name="s5_glu")
        elif kind == 1:
            q = _matmul(h, fox_w_qkvf, j, n_out=d, col_off=0, name="fox_q")
            k = _matmul(h, fox_w_qkvf, j, n_out=d, col_off=d, name="fox_k")
            v = _matmul(h, fox_w_qkvf, j, n_out=d, col_off=2 * d, name="fox_v")
            w_f = fox_w_qkvf[:, :, 3 * d:]
            lf = _matmul(h, w_f, j, n_out=heads, bias=fox_b_f, epilogue="logsigmoid", name="fox_f")
            lf_prompt = lf[:rp].reshape(bp, seq, heads)
            lf_sample = lf[rp:].reshape(nbs, dseq, heads)
            cum_p = _cumsum_time(lf_prompt)
            cum_s = _cumsum_time(jnp.concatenate([cache_fox_logf[j].astype(F32), lf_sample], axis=1))
            o_p = _flash_prefill(q, k, v, cum_p[0].T.reshape(heads, 1, seq), t_len=seq, heads=heads, hd=hd)
            o = _fox_decode(q, k, v, cache_fox_k.reshape(cache_fox_k.shape[:3] + (d,)),
                            cache_fox_v.reshape(cache_fox_v.shape[:3] + (d,)),
                            cum_s.transpose(0, 2, 1).reshape(nbs, heads, 1, past + dseq),
                            row_off=rp, nseq=nbs, s_len=dseq, heads=heads, hd=hd, layer=j, prev=[o_p])
            out = _matmul(o, fox_w_o, j, n_out=d, name="fox_o")
            k_p.append(k[:rp].reshape(bp, seq, heads, hd))
            v_p.append(v[:rp].reshape(bp, seq, heads, hd))
            lf_p.append(lf_prompt)
            k_s.append(k[rp:].reshape(nbs, dseq, heads, hd))
            v_s.append(v[rp:].reshape(nbs, dseq, heads, hd))
            lf_s.append(lf_sample)
        else:
            proj = _matmul(h, conv_w_in, j, n_out=3 * d, name="conv_in")
            zero_buf = jnp.zeros((bp, conv_w.shape[1] - 1, d), F32)
            outs = _conv_call(proj, conv_w, j, zero_buf, row_off=0, nseq=bp, t_len=seq, d=d, name="conv_prompt")
            outs2 = _conv_call(proj, conv_w, j, state_conv[j].astype(F32), row_off=rp, nseq=nbs, t_len=dseq,
                               d=d, prev=[outs[0]], name="conv_sample")
            conv_p.append(outs[1])
            conv_s.append(outs2[1])
            out = _matmul(outs2[0], conv_w_out, j, n_out=d, name="conv_out")

        x, h2, route = _ln_both(x, mods, geom, alpha=alpha, o=out, res_mod=(i, G1), ln_g=ln_g4, ln_b=ln_b4,
                                ln_idx=(i, 0), h_mods=(i, SC2, SH2), h_dtype=F32,
                                router_w=router_weights(i))
        y0, y1 = _moe(h2, route, moe_w_gate, moe_w_up, moe_w_down, i)
        if i + 1 < depth:
            x, h = _ln_both(x, mods, geom, alpha=alpha, o=y0, o2=y1, route_in=route, res_mod=(i, G2),
                            ln_g=ln_g4, ln_b=ln_b4, ln_idx=(i, 1), h_mods=(i + 1, SC1, SH1),
                            h_dtype=mixer_h_dtype(i + 1))
        else:
            (x,) = _ln_both(x, mods, geom, alpha=alpha, o=y0, o2=y1, route_in=route, res_mod=(i, G2),
                            ln_g=ln_g4, ln_b=ln_b4, ln_idx=(i, 1))

    y_prompt = x[:rp].reshape(bp, seq, d)
    y_sample = x[rp:].reshape(nbs, dseq, d)
    return (y_prompt, y_sample, jnp.stack(re_p), jnp.stack(im_p), jnp.stack(k_p), jnp.stack(v_p),
            jnp.stack(lf_p), jnp.stack(conv_p), jnp.stack(re_s), jnp.stack(im_s), jnp.stack(k_s),
            jnp.stack(v_s), jnp.stack(lf_s), jnp.stack(conv_s))
```

```python
import functools

import jax
import jax.numpy as jnp
from jax import lax
from jax.experimental import pallas as pl
from jax.experimental.pallas import tpu as pltpu

LN_EPS = 1e-5
N_MIXERS = 3
LOG2E = 1.4426950408889634

V7X_VMEM_BYTES = 64 * 1024 * 1024
VMEM_LIMIT_BYTES = V7X_VMEM_BYTES - 8 * 1024 * 1024
LANES = 128
SUBLANES = 8

F32 = jnp.float32
BF16 = jnp.bfloat16


def _pick(n, candidates):
    for c in candidates:
        if c <= n and n % c == 0:
            return c
    return n


def _cparams(sem):
    return pltpu.CompilerParams(dimension_semantics=sem, vmem_limit_bytes=VMEM_LIMIT_BYTES)


def _drop_aliased(kernel, n_in, n_alias):
    def wrapped(*refs):
        kernel(*(refs[:n_in] + refs[n_in + n_alias:]))
    return wrapped


def _with_prev(args, in_specs, prev):
    aliases = {}
    n_in = len(args)
    if prev is not None:
        for k, p in enumerate(prev):
            args.append(p)
            in_specs.append(pl.BlockSpec(memory_space=pl.ANY))
            aliases[n_in + k] = k
    return n_in, aliases


def _mm_kernel(*refs, nk, glu, has_bias, epilogue, lhs_silu):
    it = iter(refs)
    x_ref = next(it)
    w_ref = next(it)
    w2_ref = next(it) if glu else None
    b_ref = next(it) if has_bias else None
    b2_ref = next(it) if (has_bias and glu) else None
    o_ref = next(it)
    acc_ref = next(it)
    acc2_ref = next(it) if glu else None
    k = pl.program_id(2)

    def partial_sums(first):
        x = x_ref[...]
        if lhs_silu:
            xf = x.astype(F32)
            x = xf * jax.nn.sigmoid(xf)
        xb = x.astype(BF16)
        z = jnp.dot(xb, w_ref[...].astype(BF16), preferred_element_type=F32)
        z2 = jnp.dot(xb, w2_ref[...].astype(BF16), preferred_element_type=F32) if glu else None
        if not first:
            z = acc_ref[...] + z
            if glu:
                z2 = acc2_ref[...] + z2
        return z, z2

    def keep(z, z2):
        acc_ref[...] = z
        if glu:
            acc2_ref[...] = z2

    def finish(z, z2):
        if has_bias:
            z = z + b_ref[...]
        if glu:
            if has_bias:
                z2 = z2 + b2_ref[...]
            z = z * jax.nn.sigmoid(z2)
        if epilogue == "logsigmoid":
            z = jax.nn.log_sigmoid(z)
        o_ref[...] = z.astype(o_ref.dtype)

    if nk == 1:
        finish(*partial_sums(True))
        return

    @pl.when(k == 0)
    def _():
        keep(*partial_sums(True))

    if nk > 2:
        @pl.when((k > 0) & (k < nk - 1))
        def _():
            keep(*partial_sums(False))

    @pl.when(k == nk - 1)
    def _():
        finish(*partial_sums(False))


def _matmul(x, w, layer, *, n_out, col_off=0, bias=None, glu=False, epilogue=None,
            lhs_silu=False, out_dtype=F32, name="mm"):
    m, kdim = x.shape
    nw = w.shape[-1]
    if m <= 64:
        tm = m
        tn = _pick(n_out, (2048, 1024, 512, 256, 128))
        tk = _pick(kdim, (1024, 512, 256, 128))
    else:
        tm = _pick(m, (1408, 1024, 768, 512, 256, 128, 64, 32, 16, 8))
        tn = _pick(n_out, (512, 256, 128) if glu else (1024, 512, 256, 128))
        tk = _pick(kdim, (1024, 512, 256, 128))
    assert col_off % tn == 0 and n_out % tn == 0
    nk = kdim // tk
    coff = col_off // tn
    goff = (col_off + n_out) // tn
    in_specs = [pl.BlockSpec((tm, tk), lambda i, j, k: (i, k)),
                pl.BlockSpec((None, tk, tn), lambda i, j, k: (layer, k, j + coff))]
    args = [x, w]
    if glu:
        in_specs.append(pl.BlockSpec((None, tk, tn), lambda i, j, k: (layer, k, j + goff)))
        args.append(w)
    if bias is not None:
        b3 = bias.reshape(bias.shape[0], 1, nw)
        in_specs.append(pl.BlockSpec((None, 1, tn), lambda i, j, k: (layer, 0, j + coff)))
        args.append(b3)
        if glu:
            in_specs.append(pl.BlockSpec((None, 1, tn), lambda i, j, k: (layer, 0, j + goff)))
            args.append(b3)
    scratch = [pltpu.VMEM((tm, tn), F32)]
    if glu:
        scratch.append(pltpu.VMEM((tm, tn), F32))
    kern = functools.partial(_mm_kernel, nk=nk, glu=glu, has_bias=bias is not None,
                             epilogue=epilogue, lhs_silu=lhs_silu)
    return pl.pallas_call(
        kern,
        out_shape=jax.ShapeDtypeStruct((m, n_out), out_dtype),
        grid=(m // tm, n_out // tn, nk),
        in_specs=in_specs,
        out_specs=pl.BlockSpec((tm, tn), lambda i, j, k: (i, j)),
        scratch_shapes=scratch,
        compiler_params=_cparams(("parallel", "parallel", "arbitrary")),
        name=name,
    )(*args)


def _expand_mod(ref, nb, rpm):
    v = ref[...]
    d = v.shape[-1]
    if nb == 1:
        return v.reshape(1, d)
    return jnp.broadcast_to(v, (nb, rpm, d)).reshape(nb * rpm, d)


def _route(logits):
    tm = logits.shape[0]
    lane = lax.broadcasted_iota(jnp.int32, (tm, LANES), 1)
    neg = jnp.float32(-jnp.inf)
    big = jnp.int32(LANES)
    glog = jnp.where(lane < 8, logits, neg)
    gmax = jnp.max(glog, axis=1, keepdims=True)
    gidx = jnp.min(jnp.where(glog == gmax, lane, big), axis=1, keepdims=True)
    gsum = jnp.sum(jnp.exp(glog - gmax), axis=1, keepdims=True)
    g_w = 1.0 / gsum
    emask = (lane >= 8) & (lane < 72) & (((lane - 8) >> 3) == gidx)
    elog = jnp.where(emask, logits, neg)
    emax = jnp.max(elog, axis=1, keepdims=True)
    i1 = jnp.min(jnp.where(elog == emax, lane, big), axis=1, keepdims=True)
    elog2 = jnp.where(lane == i1, neg, elog)
    emax2 = jnp.max(elog2, axis=1, keepdims=True)
    i2 = jnp.min(jnp.where(elog2 == emax2, lane, big), axis=1, keepdims=True)
    esum = jnp.sum(jnp.exp(elog - emax), axis=1, keepdims=True)
    p1 = 1.0 / esum
    p2 = jnp.exp(emax2 - emax) / esum
    psum = p1 + p2
    gate0 = g_w * p1 / psum
    gate1 = g_w * p2 / psum
    return jnp.where(lane == 0, (i1 - 8).astype(F32),
                     jnp.where(lane == 1, (i2 - 8).astype(F32),
                               jnp.where(lane == 2, gate0, jnp.where(lane == 3, gate1, 0.0))))


def _ln_kernel(*refs, alpha, nb, rpm, first, combine, emit_h, router):
    it = iter(refs)
    x_ref = next(it)
    if not first:
        o_ref = next(it)
        if combine:
            o2_ref = next(it)
            rt_in_ref = next(it)
        g_ref = next(it)
        lg_ref = next(it)
        lb_ref = next(it)
    if emit_h:
        sc_ref = next(it)
        sh_ref = next(it)
    if router:
        whi_ref = next(it)
        wlo_ref = next(it)
        br_ref = next(it)
    xo_ref = None if first else next(it)
    h_ref = next(it) if emit_h else None
    rt_ref = next(it) if router else None

    x = x_ref[...].astype(F32)
    if not first:
        o = o_ref[...].astype(F32)
        if combine:
            rt = rt_in_ref[...]
            o = rt[:, 2:3] * o + rt[:, 3:4] * o2_ref[...].astype(F32)
        g = _expand_mod(g_ref, nb, rpm)
        v = alpha * x + (1.0 + g) * o
        mu = jnp.mean(v, axis=-1, keepdims=True)
        vc = v - mu
        var = jnp.mean(vc * vc, axis=-1, keepdims=True)
        x = vc * lax.rsqrt(var + LN_EPS) * lg_ref[...].reshape(1, -1) + lb_ref[...].reshape(1, -1)
        xo_ref[...] = x
    if emit_h:
        sc = _expand_mod(sc_ref, nb, rpm)
        sh = _expand_mod(sh_ref, nb, rpm)
        h = x * (1.0 + sc) + sh
        h_ref[...] = h.astype(h_ref.dtype)
        if router:
            h_hi = h.astype(BF16)
            h_lo = (h - h_hi.astype(F32)).astype(BF16)
            whi = whi_ref[...]
            logits = (jnp.dot(h_hi, whi, preferred_element_type=F32)
                      + jnp.dot(h_lo, whi, preferred_element_type=F32)
                      + jnp.dot(h_hi, wlo_ref[...], preferred_element_type=F32)
                      + br_ref[...])
            rt_ref[...] = _route(logits)


def _ln_call(x, mods, *, row_off, rows, rpm, seq0, alpha=1.0, o=None, o2=None, o2_row_off=0, route_in=None,
             res_mod=None, ln_g=None, ln_b=None, ln_idx=None, h_mods=None,
             h_dtype=None, router_w=None, prev=None, name="ln"):
    rtot, d = x.shape
    first = o is None
    combine = o2 is not None
    emit_h = h_dtype is not None
    router = router_w is not None
    tm = _pick(rows, (128, 64, 32, 16, 8))
    if tm > rpm:
        assert tm % rpm == 0 and rpm % SUBLANES == 0
        nb = tm // rpm
        assert seq0 % nb == 0
    else:
        assert rpm % tm == 0
        nb = 1
    assert row_off % tm == 0 and rows % tm == 0
    boff = row_off // tm

    def row_map(i):
        return (i + boff, 0)

    def mod_spec(lyr, chunk):
        if nb == 1:
            return pl.BlockSpec((None, 1, 1, d), lambda i: (lyr, seq0 + (i * tm) // rpm, 0, chunk))
        return pl.BlockSpec((None, nb, 1, d), lambda i: (lyr, seq0 // nb + i, 0, chunk))

    row_spec = pl.BlockSpec((tm, d), row_map)
    in_specs = [row_spec]
    args = [x]
    if not first:
        in_specs.append(row_spec)
        args.append(o)
        if combine:
            assert o2_row_off % tm == 0
            in_specs += [pl.BlockSpec((tm, d), lambda i: (i + boff + o2_row_off // tm, 0)),
                         pl.BlockSpec((tm, LANES), row_map)]
            args += [o2, route_in]
        in_specs.append(mod_spec(*res_mod))
        args.append(mods)
        in_specs += [pl.BlockSpec((None, None, 1, d), lambda i: (ln_idx[0], ln_idx[1], 0, 0))] * 2
        args += [ln_g, ln_b]
    if emit_h:
        in_specs += [mod_spec(h_mods[0], h_mods[1]), mod_spec(h_mods[0], h_mods[2])]
        args += [mods, mods]
    if router:
        whi, wlo, br = router_w
        in_specs += [pl.BlockSpec((d, LANES), lambda i: (0, 0)),
                     pl.BlockSpec((d, LANES), lambda i: (0, 0)),
                     pl.BlockSpec((1, LANES), lambda i: (0, 0))]
        args += [whi, wlo, br]
    out_shape = []
    out_specs = []
    if not first:
        out_shape.append(jax.ShapeDtypeStruct((rtot, d), F32))
        out_specs.append(row_spec)
    if emit_h:
        out_shape.append(jax.ShapeDtypeStruct((rtot, d), h_dtype))
        out_specs.append(row_spec)
    if router:
        out_shape.append(jax.ShapeDtypeStruct((rtot, LANES), F32))
        out_specs.append(pl.BlockSpec((tm, LANES), row_map))
    n_in, aliases = _with_prev(args, in_specs, prev)
    kern = functools.partial(_ln_kernel, alpha=alpha, nb=nb, rpm=rpm, first=first,
                             combine=combine, emit_h=emit_h, router=router)
    outs = pl.pallas_call(
        _drop_aliased(kern, n_in, len(aliases)),
        out_shape=out_shape,
        grid=(rows // tm,),
        in_specs=in_specs,
        out_specs=out_specs,
        input_output_aliases=aliases,
        compiler_params=_cparams(("parallel",)),
        name=name,
    )(*args)
    return list(outs)


def _ln_both(x, mods, geom, **kw):
    outs = _ln_call(x, mods, row_off=0, rows=geom["rp"], rpm=geom["seq"], seq0=geom["nbs"],
                    name="ln_prompt", **kw)
    return _ln_call(x, mods, row_off=geom["rp"], rows=geom["rs"], rpm=geom["dseq"], seq0=0,
                    prev=outs, name="ln_sample", **kw)


S5_GC = 16
S5_P = 64
S5_GPB = LANES // S5_GC
S5_CB = S5_GPB * S5_P // LANES


def _s5_kernel(u_ref, win_ref, wout_ref, are_ref, aim_ref, d_ref, h0r_ref, h0i_ref,
               y_ref, hr_ref, hi_ref, bre, bim, sre, sim, *, tl, pitch, nkb):
    t = pl.program_id(2)
    ncb = nkb * S5_CB
    half = S5_CB * LANES

    @pl.when(t == 0)
    def _():
        sre[...] = h0r_ref[...]
        sim[...] = h0i_ref[...]

    u = u_ref[...]
    ub = u.astype(BF16)
    for kk in range(nkb):
        bu = jnp.dot(ub[:, kk * LANES:(kk + 1) * LANES], win_ref[kk], preferred_element_type=F32)
        for j in range(S5_CB):
            c = kk * S5_CB + j
            bre[pl.ds(c * pitch, tl), :] = bu[:, j * LANES:(j + 1) * LANES]
            bim[pl.ds(c * pitch, tl), :] = bu[:, half + j * LANES:half + (j + 1) * LANES]

    nog = ncb // SUBLANES
    a_r = [are_ref[pl.ds(o * SUBLANES, SUBLANES), :] for o in range(nog)]
    a_i = [aim_ref[pl.ds(o * SUBLANES, SUBLANES), :] for o in range(nog)]
    s0 = tuple(sre[pl.ds(o * SUBLANES, SUBLANES), :] for o in range(nog)) + \
        tuple(sim[pl.ds(o * SUBLANES, SUBLANES), :] for o in range(nog))

    def step(tt, carry):
        new_r, new_i = [], []
        for o in range(nog):
            s_r, s_i = carry[o], carry[nog + o]
            idx = pl.ds(o * SUBLANES * pitch + tt, SUBLANES, stride=pitch)
            n_r = a_r[o] * s_r - a_i[o] * s_i + bre[idx, :]
            n_i = a_r[o] * s_i + a_i[o] * s_r + bim[idx, :]
            bre[idx, :] = n_r
            bim[idx, :] = n_i
            new_r.append(n_r)
            new_i.append(n_i)
        return tuple(new_r) + tuple(new_i)

    fin = lax.fori_loop(0, tl, step, s0)
    for o in range(nog):
        sre[pl.ds(o * SUBLANES, SUBLANES), :] = fin[o]
        sim[pl.ds(o * SUBLANES, SUBLANES), :] = fin[nog + o]
    hr_ref[...] = sre[...]
    hi_ref[...] = sim[...]

    ys = []
    for kk in range(nkb):
        parts = [bre[pl.ds((kk * S5_CB + j) * pitch, tl), :] for j in range(S5_CB)]
        parts += [bim[pl.ds((kk * S5_CB + j) * pitch, tl), :] for j in range(S5_CB)]
        lhs = jnp.concatenate(parts, axis=1).astype(BF16)
        ys.append(jnp.dot(lhs, wout_ref[kk], preferred_element_type=F32))
    y = jnp.concatenate(ys, axis=1) if nkb > 1 else ys[0]
    yy = y + d_ref[...] * u
    y_ref[...] = jax.nn.gelu(yy).astype(y_ref.dtype)


def _s5_weights(lam_re, lam_im, log_dt, b_re, b_im, c_re, c_im):
    g, p, gc = b_re.shape
    assert gc == S5_GC and p == S5_P
    dt = jnp.exp(log_dt.astype(F32))[:, None]
    mag = jnp.exp(lam_re * dt)
    ab_re = mag * jnp.cos(lam_im * dt)
    ab_im = mag * jnp.sin(lam_im * dt)
    den = lam_re * lam_re + lam_im * lam_im
    nr = ab_re - 1.0
    z_re = (nr * lam_re + ab_im * lam_im) / den
    z_im = (ab_im * lam_re - nr * lam_im) / den
    bb_re = z_re[..., None] * b_re - z_im[..., None] * b_im
    bb_im = z_re[..., None] * b_im + z_im[..., None] * b_re
    nkb = g // S5_GPB
    eye = jnp.eye(S5_GPB, dtype=F32)

    def in_blocks(bb):
        t = bb.reshape(nkb, S5_GPB, p, gc).transpose(0, 1, 3, 2)
        w = t[:, :, :, None, :] * eye[None, :, None, :, None]
        return w.reshape(nkb, S5_GPB * gc, S5_GPB * p)

    def out_blocks(cc):
        t = cc.reshape(nkb, S5_GPB, gc, p).transpose(0, 1, 3, 2)
        w = t[:, :, :, None, :] * eye[None, :, None, :, None]
        return w.reshape(nkb, S5_GPB * p, S5_GPB * gc)

    w_in = jnp.concatenate([in_blocks(bb_re), in_blocks(bb_im)], axis=2).astype(BF16)
    w_out = jnp.concatenate([out_blocks(c_re.astype(F32)), -out_blocks(c_im.astype(F32))], axis=1).astype(BF16)
    ncols = g * p // LANES
    return w_in, w_out, ab_re.reshape(ncols, LANES), ab_im.reshape(ncols, LANES)


def _s5_call(h, wts, d_skip, h0_re, h0_im, *, row_off, nseq, t_len, prev=None, name="s5"):
    rtot, d = h.shape
    w_in, w_out, a_re, a_im = wts
    ck = _pick(d, (1024, 512, 256))
    nkb = ck // LANES
    ncb = nkb * S5_CB
    assert ncb % SUBLANES == 0
    nkg = d // ck
    tl = _pick(t_len, (256, 128, 64, 32, 16, 8))
    pitch = tl + SUBLANES if (tl // SUBLANES) % 2 == 0 else tl + 2 * SUBLANES
    nt = t_len // tl
    assert row_off % tl == 0
    boff = row_off // tl
    kern = functools.partial(_s5_kernel, tl=tl, pitch=pitch, nkb=nkb)
    in_specs = [
        pl.BlockSpec((tl, ck), lambda kg, b, t: (boff + b * nt + t, kg)),
        pl.BlockSpec((nkb, LANES, 2 * S5_CB * LANES), lambda kg, b, t: (kg, 0, 0)),
        pl.BlockSpec((nkb, 2 * S5_CB * LANES, LANES), lambda kg, b, t: (kg, 0, 0)),
        pl.BlockSpec((ncb, LANES), lambda kg, b, t: (kg, 0)),
        pl.BlockSpec((ncb, LANES), lambda kg, b, t: (kg, 0)),
        pl.BlockSpec((1, ck), lambda kg, b, t: (0, kg)),
        pl.BlockSpec((None, ncb, LANES), lambda kg, b, t: (b, kg, 0)),
        pl.BlockSpec((None, ncb, LANES), lambda kg, b, t: (b, kg, 0)),
    ]
    args = [h, w_in, w_out, a_re, a_im, d_skip, h0_re, h0_im]
    n_in, aliases = _with_prev(args, in_specs, prev)
    nstate = a_re.shape[0]
    out_shape = [jax.ShapeDtypeStruct((rtot, d), BF16),
                 jax.ShapeDtypeStruct((nseq, nstate, LANES), F32),
                 jax.ShapeDtypeStruct((nseq, nstate, LANES), F32)]
    out_specs = [pl.BlockSpec((tl, ck), lambda kg, b, t: (boff + b * nt + t, kg)),
                 pl.BlockSpec((None, ncb, LANES), lambda kg, b, t: (b, kg, 0)),
                 pl.BlockSpec((None, ncb, LANES), lambda kg, b, t: (b, kg, 0))]
    return pl.pallas_call(
        _drop_aliased(kern, n_in, len(aliases)),
        out_shape=out_shape,
        grid=(nkg, nseq, nt),
        in_specs=in_specs,
        out_specs=out_specs,
        scratch_shapes=[pltpu.VMEM((ncb * pitch, LANES), F32), pltpu.VMEM((ncb * pitch, LANES), F32),
                        pltpu.VMEM((ncb, LANES), F32), pltpu.VMEM((ncb, LANES), F32)],
        input_output_aliases=aliases,
        compiler_params=_cparams(("arbitrary", "arbitrary", "arbitrary")),
        name=name,
    )(*args)


def _cumsum_kernel(x_ref, o_ref, carry, *, tl):
    t = pl.program_id(1)

    @pl.when(t == 0)
    def _():
        carry[...] = jnp.zeros_like(carry)

    x = x_ref[...]
    row = lax.broadcasted_iota(jnp.int32, (tl, tl), 0)
    col = lax.broadcasted_iota(jnp.int32, (tl, tl), 1)
    tri = (col <= row).astype(BF16)
    x_hi = x.astype(BF16)
    r1 = x - x_hi.astype(F32)
    x_mid = r1.astype(BF16)
    x_lo = (r1 - x_mid.astype(F32)).astype(BF16)
    c = (jnp.dot(tri, x_hi, preferred_element_type=F32)
         + jnp.dot(tri, x_mid, preferred_element_type=F32)
         + jnp.dot(tri, x_lo, preferred_element_type=F32)) + carry[...]
    o_ref[...] = c
    carry[...] = c[tl - 1:tl, :]


def _cumsum_time(x):
    b, t, hh = x.shape
    tl = _pick(t, (512, 256, 128)) if t % 128 == 0 else t
    return pl.pallas_call(
        functools.partial(_cumsum_kernel, tl=tl),
        out_shape=jax.ShapeDtypeStruct((b, t, hh), F32),
        grid=(b, t // tl),
        in_specs=[pl.BlockSpec((None, tl, hh), lambda i, j: (i, j, 0))],
        out_specs=pl.BlockSpec((None, tl, hh), lambda i, j: (i, j, 0)),
        scratch_shapes=[pltpu.VMEM((1, hh), F32)],
        compiler_params=_cparams(("arbitrary", "arbitrary")),
        name="cumsum_time",
    )(x)


def _flash_kernel(qt_ref, kt_ref, q_ref, k_ref, v_ref, ck_ref, o_ref, m_s, l_s, acc_s, *, tq, scale):
    p = pl.program_id(1)
    qi = qt_ref[p]
    kj = kt_ref[p]

    @pl.when(kj == 0)
    def _():
        m_s[...] = jnp.full_like(m_s, -jnp.inf)
        l_s[...] = jnp.zeros_like(l_s)
        acc_s[...] = jnp.zeros_like(acc_s)

    def scores():
        q = (q_ref[...] * (scale * LOG2E)).astype(BF16)
        k = k_ref[...].astype(BF16)
        return (lax.dot_general(q, k, (((1,), (1,)), ((), ())), preferred_element_type=F32)
                - ck_ref[...] * LOG2E)

    def update(s):
        m_prev = m_s[...]
        m_new = jnp.maximum(m_prev, jnp.max(s, axis=1, keepdims=True))
        a = jnp.exp2(m_prev - m_new)
        pexp = jnp.exp2(s - m_new)
        l_s[...] = a * l_s[...] + jnp.sum(pexp, axis=1, keepdims=True)
        acc_s[...] = a * acc_s[...] + jnp.dot(pexp.astype(BF16), v_ref[...].astype(BF16),
                                              preferred_element_type=F32)
        m_s[...] = m_new

    @pl.when(kj < qi)
    def _():
        update(scores())

    @pl.when(kj == qi)
    def _():
        row = lax.broadcasted_iota(jnp.int32, (tq, tq), 0)
        col = lax.broadcasted_iota(jnp.int32, (tq, tq), 1)
        update(jnp.where(col <= row, scores(), -jnp.inf))
        o_ref[...] = (acc_s[...] / l_s[...]).astype(o_ref.dtype)


def _flash_prefill(q, k, v, cum_t, *, t_len, heads, hd):
    rtot, d = q.shape
    tq = _pick(t_len, (1024, 512, 256, 128))
    nq = t_len // tq
    pairs = [(i, j) for i in range(nq) for j in range(i + 1)]
    qt = jnp.asarray([a for a, _ in pairs], jnp.int32)
    kt = jnp.asarray([b for _, b in pairs], jnp.int32)
    grid_spec = pltpu.PrefetchScalarGridSpec(
        num_scalar_prefetch=2,
        grid=(heads, len(pairs)),
        in_specs=[pl.BlockSpec((tq, hd), lambda h, p, qt, kt: (qt[p], h)),
                  pl.BlockSpec((tq, hd), lambda h, p, qt, kt: (kt[p], h)),
                  pl.BlockSpec((tq, hd), lambda h, p, qt, kt: (kt[p], h)),
                  pl.BlockSpec((None, 1, tq), lambda h, p, qt, kt: (h, 0, kt[p]))],
        out_specs=pl.BlockSpec((tq, hd), lambda h, p, qt, kt: (qt[p], h)),
        scratch_shapes=[pltpu.VMEM((tq, 1), F32), pltpu.VMEM((tq, 1), F32), pltpu.VMEM((tq, hd), F32)],
    )
    return pl.pallas_call(
        functools.partial(_flash_kernel, tq=tq, scale=float(hd) ** -0.5),
        out_shape=jax.ShapeDtypeStruct((rtot, d), BF16),
        grid_spec=grid_spec,
        compiler_params=_cparams(("arbitrary", "arbitrary")),
        name="fox_prefill",
    )(qt, kt, q, k, v, cum_t)


def _decode_kernel(q_ref, kn_ref, vn_ref, kc_ref, vc_ref, ck_ref, o_ref, *, past, s_len, scale, hpb, hd):
    nt = (((1,), (1,)), ((), ()))
    row = lax.broadcasted_iota(jnp.int32, (s_len, s_len), 0)
    col = lax.broadcasted_iota(jnp.int32, (s_len, s_len), 1)
    for hh in range(hpb):
        cols = slice(hh * hd, (hh + 1) * hd)
        q = (q_ref[:, cols] * scale).astype(BF16)
        ck = ck_ref[hh]
        kc = kc_ref[:, cols].astype(BF16)
        vc = vc_ref[:, cols].astype(BF16)
        s_c = lax.dot_general(q, kc, nt, preferred_element_type=F32) - ck[:, :past]
        s_n = lax.dot_general(q, kn_ref[:, cols].astype(BF16), nt, preferred_element_type=F32) - ck[:, past:]
        s_n = jnp.where(col <= row, s_n, -jnp.inf)
        m = jnp.maximum(jnp.max(s_c, axis=1, keepdims=True), jnp.max(s_n, axis=1, keepdims=True))
        p_c = jnp.exp(s_c - m)
        p_n = jnp.exp(s_n - m)
        l = jnp.sum(p_c, axis=1, keepdims=True) + jnp.sum(p_n, axis=1, keepdims=True)
        acc = (jnp.dot(p_c.astype(BF16), vc, preferred_element_type=F32)
               + jnp.dot(p_n.astype(BF16), vn_ref[:, cols].astype(BF16), preferred_element_type=F32))
        o_ref[:, cols] = (acc / l).astype(o_ref.dtype)


def _fox_decode(q, k, v, k_cache, v_cache, cum_t, *, row_off, nseq, s_len, heads, hd, layer, prev):
    rtot, d = q.shape
    past = k_cache.shape[2]
    assert row_off % s_len == 0 and past % LANES == 0
    boff = row_off // s_len
    hpb = _pick(heads, (4, 2, 1))
    new_spec = pl.BlockSpec((s_len, hpb * hd), lambda b, h: (boff + b, h))
    cache_spec = pl.BlockSpec((None, None, past, hpb * hd), lambda b, h: (layer, b, 0, h))
    in_specs = [new_spec, new_spec, new_spec, cache_spec, cache_spec,
                pl.BlockSpec((None, hpb, 1, past + s_len), lambda b, h: (b, h, 0, 0))]
    args = [q, k, v, k_cache, v_cache, cum_t]
    n_in, aliases = _with_prev(args, in_specs, prev)
    kern = functools.partial(_decode_kernel, past=past, s_len=s_len, scale=float(hd) ** -0.5, hpb=hpb, hd=hd)
    return pl.pallas_call(
        _drop_aliased(kern, n_in, len(aliases)),
        out_shape=[jax.ShapeDtypeStruct((rtot, d), BF16)],
        grid=(nseq, heads // hpb),
        in_specs=in_specs,
        out_specs=[new_spec],
        input_output_aliases=aliases,
        compiler_params=_cparams(("parallel", "parallel")),
        name="fox_decode",
    )(*args)[0]


def _conv_kernel(gb_ref, gc_ref, hin_ref, cw_ref, buf_ref, o_ref, zl_ref, carry, *, tl):
    t = pl.program_id(2)

    @pl.when(t == 0)
    def _():
        carry[...] = buf_ref[...]

    z = gc_ref[...] * hin_ref[...]
    c0 = carry[0:1, :]
    c1 = carry[1:2, :]
    row = lax.broadcasted_iota(jnp.int32, z.shape, 0)
    zm1 = jnp.where(row == 0, c1, pltpu.roll(z, 1, axis=0))
    zm2 = jnp.where(row == 0, c0, jnp.where(row == 1, c1, pltpu.roll(z, 2, axis=0)))
    cw = cw_ref[...]
    acc = cw[0:1, :] * zm2 + cw[1:2, :] * zm1 + cw[2:3, :] * z
    o_ref[...] = (gb_ref[...] * acc).astype(o_ref.dtype)
    last = z[tl - 2:tl, :]
    carry[...] = last
    zl_ref[...] = last


def _conv_call(proj, conv_w, layer, buf, *, row_off, nseq, t_len, d, prev=None, name="conv"):
    rtot = proj.shape[0]
    assert conv_w.shape[1] == 3 and t_len >= 2
    tn = _pick(d, (1024, 512, 256, 128))
    tl = _pick(t_len, (256, 128, 64, 32, 16, 8))
    nt = t_len // tl
    nd = d // tn
    assert row_off % tl == 0
    boff = row_off // tl

    def pspec(part):
        return pl.BlockSpec((tl, tn), lambda b, j, t: (boff + b * nt + t, part * nd + j))

    in_specs = [pspec(0), pspec(1), pspec(2),
                pl.BlockSpec((None, 3, tn), lambda b, j, t: (layer, 0, j)),
                pl.BlockSpec((None, 2, tn), lambda b, j, t: (b, 0, j))]
    args = [proj, proj, proj, conv_w, buf]
    n_in, aliases = _with_prev(args, in_specs, prev)
    return pl.pallas_call(
        _drop_aliased(functools.partial(_conv_kernel, tl=tl), n_in, len(aliases)),
        out_shape=[jax.ShapeDtypeStruct((rtot, d), BF16), jax.ShapeDtypeStruct((nseq, 2, d), F32)],
        grid=(nseq, nd, nt),
        in_specs=in_specs,
        out_specs=[pl.BlockSpec((tl, tn), lambda b, j, t: (boff + b * nt + t, j)),
                   pl.BlockSpec((None, 2, tn), lambda b, j, t: (b, 0, j))],
        scratch_shapes=[pltpu.VMEM((2, tn), F32)],
        input_output_aliases=aliases,
        compiler_params=_cparams(("arbitrary", "arbitrary", "arbitrary")),
        name=name,
    )(*args)


def _gather_kernel(idx_ref, src_ref, o_ref, sem, *, tg):
    base = pl.program_id(0) * tg

    def row_copy(r, row):
        return pltpu.make_async_copy(src_ref.at[pl.ds(row, 1)], o_ref.at[pl.ds(r, 1)], sem)

    def start(r, c):
        row_copy(r, idx_ref[base + r]).start()
        return c

    def wait(r, c):
        row_copy(r, 0).wait()
        return c

    lax.fori_loop(0, tg, start, 0, unroll=8)
    lax.fori_loop(0, tg, wait, 0, unroll=8)


def _gather_rows(src, idx, name="gather_rows"):
    m = idx.shape[0]
    d = src.shape[1]
    tg = _pick(m, (256, 128, 64, 32, 16, 8))
    grid_spec = pltpu.PrefetchScalarGridSpec(
        num_scalar_prefetch=1,
        grid=(m // tg,),
        in_specs=[pl.BlockSpec(memory_space=pl.ANY)],
        out_specs=pl.BlockSpec((tg, d), lambda i, idx: (i, 0)),
        scratch_shapes=[pltpu.SemaphoreType.DMA],
    )
    return pl.pallas_call(
        functools.partial(_gather_kernel, tg=tg),
        out_shape=jax.ShapeDtypeStruct((m, d), src.dtype),
        grid_spec=grid_spec,
        compiler_params=_cparams(("arbitrary",)),
        name=name,
    )(idx, src)


def _expert_kernel(be_ref, nu_ref, ss_ref, nr_ref, tok_ref, h_ref, wg_ref, wu_ref, wd_ref, o_ref,
                   xbuf, sem, hg, hu, hb, *, nkt, tk):
    b = pl.program_id(0)
    j = pl.program_id(1)
    nu = nu_ref[0]
    slot = lax.rem(b, 2)

    def row_copy(s, r, tok):
        return pltpu.make_async_copy(h_ref.at[pl.ds(tok, 1)], xbuf.at[s, pl.ds(r, 1)], sem.at[s])

    def start_block(bb, s):
        base = ss_ref[bb]

        def body(r, c):
            row_copy(s, r, tok_ref[base + r]).start()
            return c
        lax.fori_loop(0, nr_ref[bb], body, 0)

    def wait_block(bb, s):
        def body(r, c):
            row_copy(s, r, 0).wait()
            return c
        lax.fori_loop(0, nr_ref[bb], body, 0)

    @pl.when((b == 0) & (j == 0))
    def _():
        xbuf[...] = jnp.zeros_like(xbuf)
        start_block(0, 0)

    @pl.when((b < nu) & (j == 0))
    def _():
        wait_block(b, slot)

        @pl.when(b + 1 < nu)
        def _():
            start_block(b + 1, 1 - slot)

    @pl.when(b < nu)
    def _():
        for jj in range(nkt):
            @pl.when(j == jj)
            def _(jj=jj):
                xb = xbuf[slot, :, jj * tk:(jj + 1) * tk].astype(BF16)
                pg = jnp.dot(xb, wg_ref[...].astype(BF16), preferred_element_type=F32)
                pu = jnp.dot(xb, wu_ref[...].astype(BF16), preferred_element_type=F32)
                if jj > 0:
                    pg = hg[...] + pg
                    pu = hu[...] + pu
                if jj < nkt - 1:
                    hg[...] = pg
                    hu[...] = pu
                else:
                    hb[...] = (pg * jax.nn.sigmoid(pg) * pu).astype(BF16)

        @pl.when(j >= nkt)
        def _():
            o_ref[...] = jnp.dot(hb[...], wd_ref[...].astype(BF16), preferred_element_type=F32)


def _expert_call(h, plan, w_gate, w_up, w_down, layer, *, tm):
    block_expert, n_used, src_start, nrows, sorted_tok = plan
    d = h.shape[1]
    de = w_gate.shape[-1]
    nb = block_expert.shape[0]
    rows = nb * tm
    tk = _pick(d, (1024, 512, 256, 128))
    tn = tk
    nkt = d // tk
    nnt = d // tn
    last_n = nnt - 1

    def blk(b, nu):
        return jnp.minimum(b, nu[0] - 1)

    def kidx(j):
        return jnp.minimum(j, nkt - 1)

    def nidx(b, j, nu):
        return jnp.where(b < nu[0], jnp.maximum(j - nkt, 0), last_n)

    def wspec(shape, imap):
        return pl.BlockSpec(shape, lambda b, j, be, nu, ss, nr, tok: imap(b, j, be, nu))

    grid_spec = pltpu.PrefetchScalarGridSpec(
        num_scalar_prefetch=5,
        grid=(nb, nkt + nnt),
        in_specs=[
            pl.BlockSpec(memory_space=pl.ANY),
            wspec((None, None, tk, de), lambda b, j, be, nu: (layer, be[blk(b, nu)], kidx(j), 0)),
            wspec((None, None, tk, de), lambda b, j, be, nu: (layer, be[blk(b, nu)], kidx(j), 0)),
            wspec((None, None, de, tn), lambda b, j, be, nu: (layer, be[blk(b, nu)], 0, nidx(b, j, nu))),
        ],
        out_specs=wspec((tm, tn), lambda b, j, be, nu: (blk(b, nu), nidx(b, j, nu))),
        scratch_shapes=[pltpu.VMEM((2, tm, d), F32), pltpu.SemaphoreType.DMA((2,)),
                        pltpu.VMEM((tm, de), F32), pltpu.VMEM((tm, de), F32), pltpu.VMEM((tm, de), BF16)],
    )
    return pl.pallas_call(
        functools.partial(_expert_kernel, nkt=nkt, tk=tk),
        out_shape=jax.ShapeDtypeStruct((rows, d), F32),
        grid_spec=grid_spec,
        compiler_params=_cparams(("arbitrary", "arbitrary")),
        name="moe_experts",
    )(block_expert, n_used, src_start, nrows, sorted_tok, h, w_gate, w_up, w_down)


def _dispatch(ids, n_experts, tm):
    r, k = ids.shape
    a = r * k
    flat_e = ids.reshape(a)
    order = jnp.argsort(flat_e, stable=True).astype(jnp.int32)
    inv = jnp.argsort(order).astype(jnp.int32)
    onehot = flat_e[:, None] == jnp.arange(n_experts, dtype=jnp.int32)[None, :]
    counts = jnp.sum(onehot, axis=0, dtype=jnp.int32)
    padded = (counts + tm - 1) // tm * tm
    pad_end = jnp.cumsum(padded)
    pad_start = pad_end - padded
    start = jnp.cumsum(counts) - counts
    shift = jnp.sum(jnp.where(onehot, (pad_start - start)[None, :], 0), axis=1, dtype=jnp.int32)
    slot_of_assign = (inv + shift).reshape(r, k)
    nb = -(-a // tm) + n_experts
    blk_row = jnp.arange(nb, dtype=jnp.int32) * tm
    block_expert = jnp.minimum(jnp.sum(pad_end[None, :] <= blk_row[:, None], axis=1, dtype=jnp.int32),
                               n_experts - 1)
    q = blk_row - pad_start[block_expert]
    nrows = jnp.clip(counts[block_expert] - q, 0, tm).astype(jnp.int32)
    src_start = jnp.clip(start[block_expert] + q, 0, a - 1).astype(jnp.int32)
    n_used = (pad_end[-1] // tm).astype(jnp.int32).reshape(1)
    sorted_tok = order // k
    return (block_expert, n_used, src_start, nrows, sorted_tok), slot_of_assign


def _moe(h, route, w_gate, w_up, w_down, layer):
    n_experts = w_gate.shape[1]
    ids = route[:, 0:2].astype(jnp.int32)
    a = ids.size
    tm = _pick(a, (512, 256, 128, 64, 32, 16, 8))
    plan, slot_of_assign = _dispatch(ids, n_experts, tm)
    ys = _expert_call(h, plan, w_gate, w_up, w_down, layer, tm=tm)
    return _gather_rows(ys, slot_of_assign.T.reshape(a), name="moe_gather_y")


def kernel(x_prompt, x_sample, c_prompt, c_sample, state_ssm_re, state_ssm_im, cache_fox_k, cache_fox_v, cache_fox_logf, state_conv, ssm_lam_re, ssm_lam_im, ssm_log_dt, ssm_b_re, ssm_b_im, ssm_c_re, ssm_c_im, ssm_d, ssm_w_glu, ssm_b_glu, fox_w_qkvf, fox_b_f, fox_w_o, conv_w_in, conv_w, conv_w_out, ada_w, ada_b, ln_g, ln_b, moe_w_group, moe_b_group, moe_w_expert, moe_b_expert, moe_w_gate, moe_w_up, moe_w_down):
    bp, seq, d = x_prompt.shape
    nbs, dseq, _ = x_sample.shape
    depth = ada_w.shape[0]
    heads, hd = cache_fox_k.shape[3], cache_fox_k.shape[4]
    past = cache_fox_k.shape[2]
    n_groups = moe_w_group.shape[-1]
    n_experts = moe_w_expert.shape[-1]
    assert bp == 1 and n_groups == 8 and n_experts == 64 and heads * hd == d
    alpha = float((2 * depth) ** 0.25)
    rp, rs = bp * seq, nbs * dseq
    geom = dict(rp=rp, rs=rs, seq=seq, dseq=dseq, nbs=nbs)

    x = jnp.concatenate([x_prompt.reshape(rp, d), x_sample.reshape(rs, d)], axis=0)

    nmod = -(-(nbs + bp) // SUBLANES) * SUBLANES
    c_all = jnp.concatenate([c_sample, c_prompt, jnp.zeros((nmod - nbs - bp, d), F32)], axis=0)
    mods = jnp.stack([_matmul(c_all, ada_w, i, n_out=6 * d, bias=ada_b, lhs_silu=True, name="ada")
                      for i in range(depth)]).reshape(depth, nmod, 1, 6 * d)
    ln_g4 = ln_g.reshape(depth, 2, 1, d)
    ln_b4 = ln_b.reshape(depth, 2, 1, d)
    SH1, SC1, G1, SH2, SC2, G2 = range(6)

    def mixer_h_dtype(i):
        return F32 if i % N_MIXERS == 0 else BF16

    def router_weights(i):
        w = jnp.concatenate([moe_w_group[i], moe_w_expert[i],
                             jnp.zeros((d, LANES - n_groups - n_experts), F32)], axis=1)
        b = jnp.concatenate([moe_b_group[i], moe_b_expert[i],
                             jnp.zeros((LANES - n_groups - n_experts,), F32)]).reshape(1, LANES)
        w_hi = w.astype(BF16)
        w_lo = (w - w_hi.astype(F32)).astype(BF16)
        return w_hi, w_lo, b

    (h,) = _ln_both(x, mods, geom, h_mods=(0, SC1, SH1), h_dtype=mixer_h_dtype(0))

    re_p, im_p, re_s, im_s = [], [], [], []
    k_p, v_p, lf_p, k_s, v_s, lf_s = [], [], [], [], [], []
    conv_p, conv_s = [], []
    for i in range(depth):
        j = i // N_MIXERS
        kind = i % N_MIXERS
        if kind == 0:
            wts = _s5_weights(ssm_lam_re[j].astype(F32), ssm_lam_im[j].astype(F32), ssm_log_dt[j],
                              ssm_b_re[j].astype(F32), ssm_b_im[j].astype(F32), ssm_c_re[j], ssm_c_im[j])
            nstate = wts[2].shape[0]
            zero_state = jnp.zeros((bp, nstate, LANES), F32)
            d_skip = ssm_d[j].reshape(1, d)
            outs = _s5_call(h, wts, d_skip, zero_state, zero_state, row_off=0, nseq=bp, t_len=seq,
                            name="s5_prompt")
            outs2 = _s5_call(h, wts, d_skip, state_ssm_re[j].reshape(nbs, nstate, LANES),
                             state_ssm_im[j].reshape(nbs, nstate, LANES), row_off=rp, nseq=nbs,
                             t_len=dseq, prev=[outs[0]], name="s5_sample")
            gshape = ssm_lam_re.shape[1:]
            re_p.append(outs[1].reshape((bp,) + gshape))
            im_p.append(outs[2].reshape((bp,) + gshape))
            re_s.append(outs2[1].reshape((nbs,) + gshape))
            im_s.append(outs2[2].reshape((nbs,) + gshape))
            out = _matmul(outs2[0], ssm_w_glu, j, n_out=d, bias=ssm_b_glu, glu=True, name="s5_glu")
        elif kind == 1:
            q = _matmul(h, fox_w_qkvf, j, n_out=d, col_off=0, name="fox_q")
            k = _matmul(h, fox_w_qkvf, j, n_out=d, col_off=d, name="fox_k")
            v = _matmul(h, fox_w_qkvf, j, n_out=d, col_off=2 * d, name="fox_v")
            w_f = fox_w_qkvf[:, :, 3 * d:]
            lf = _matmul(h, w_f, j, n_out=heads, bias=fox_b_f, epilogue="logsigmoid", name="fox_f")
            lf_prompt = lf[:rp].reshape(bp, seq, heads)
            lf_sample = lf[rp:].reshape(nbs, dseq, heads)
            cum_p = _cumsum_time(lf_prompt)
            cum_s = _cumsum_time(jnp.concatenate([cache_fox_logf[j].astype(F32), lf_sample], axis=1))
            o_p = _flash_prefill(q, k, v, cum_p[0].T.reshape(heads, 1, seq), t_len=seq, heads=heads, hd=hd)
            o = _fox_decode(q, k, v, cache_fox_k.reshape(cache_fox_k.shape[:3] + (d,)),
                            cache_fox_v.reshape(cache_fox_v.shape[:3] + (d,)),
                            cum_s.transpose(0, 2, 1).reshape(nbs, heads, 1, past + dseq),
                            row_off=rp, nseq=nbs, s_len=dseq, heads=heads, hd=hd, layer=j, prev=[o_p])
            out = _matmul(o, fox_w_o, j, n_out=d, name="fox_o")
            k_p.append(k[:rp].reshape(bp, seq, heads, hd))
            v_p.append(v[:rp].reshape(bp, seq, heads, hd))
            lf_p.append(lf_prompt)
            k_s.append(k[rp:].reshape(nbs, dseq, heads, hd))
            v_s.append(v[rp:].reshape(nbs, dseq, heads, hd))
            lf_s.append(lf_sample)
        else:
            proj = _matmul(h, conv_w_in, j, n_out=3 * d, name="conv_in")
            zero_buf = jnp.zeros((bp, conv_w.shape[1] - 1, d), F32)
            outs = _conv_call(proj, conv_w, j, zero_buf, row_off=0, nseq=bp, t_len=seq, d=d, name="conv_prompt")
            outs2 = _conv_call(proj, conv_w, j, state_conv[j].astype(F32), row_off=rp, nseq=nbs, t_len=dseq,
                               d=d, prev=[outs[0]], name="conv_sample")
            conv_p.append(outs[1])
            conv_s.append(outs2[1])
            out = _matmul(outs2[0], conv_w_out, j, n_out=d, name="conv_out")

        x, h2, route = _ln_both(x, mods, geom, alpha=alpha, o=out, res_mod=(i, G1), ln_g=ln_g4, ln_b=ln_b4,
                                ln_idx=(i, 0), h_mods=(i, SC2, SH2), h_dtype=F32,
                                router_w=router_weights(i))
        y01 = _moe(h2, route, moe_w_gate, moe_w_up, moe_w_down, i)
        ffn = dict(alpha=alpha, o=y01, o2=y01, o2_row_off=rp + rs, route_in=route, res_mod=(i, G2),
                   ln_g=ln_g4, ln_b=ln_b4, ln_idx=(i, 1))
        if i + 1 < depth:
            x, h = _ln_both(x, mods, geom, h_mods=(i + 1, SC1, SH1), h_dtype=mixer_h_dtype(i + 1), **ffn)
        else:
            (x,) = _ln_both(x, mods, geom, **ffn)

    y_prompt = x[:rp].reshape(bp, seq, d)
    y_sample = x[rp:].reshape(nbs, dseq, d)
    return (y_prompt, y_sample, jnp.stack(re_p), jnp.stack(im_p), jnp.stack(k_p), jnp.stack(v_p),
            jnp.stack(lf_p), jnp.stack(conv_p), jnp.stack(re_s), jnp.stack(im_s), jnp.stack(k_s),
            jnp.stack(v_s), jnp.stack(lf_s), jnp.stack(conv_s))
```

```python
import functools

import jax
import jax.numpy as jnp
from jax import lax
from jax.experimental import pallas as pl
from jax.experimental.pallas import tpu as pltpu

LN_EPS = 1e-5
N_MIXERS = 3
LOG2E = 1.4426950408889634

V7X_VMEM_BYTES = 64 * 1024 * 1024
VMEM_LIMIT_BYTES = V7X_VMEM_BYTES - 8 * 1024 * 1024
LANES = 128
SUBLANES = 8

F32 = jnp.float32
BF16 = jnp.bfloat16


def _pick(n, candidates):
    for c in candidates:
        if c <= n and n % c == 0:
            return c
    return n


def _cparams(sem):
    return pltpu.CompilerParams(dimension_semantics=sem, vmem_limit_bytes=VMEM_LIMIT_BYTES)


def _drop_aliased(kernel, n_in, n_alias):
    def wrapped(*refs):
        kernel(*(refs[:n_in] + refs[n_in + n_alias:]))
    return wrapped


def _with_prev(args, in_specs, prev):
    aliases = {}
    n_in = len(args)
    if prev is not None:
        for k, p in enumerate(prev):
            args.append(p)
            in_specs.append(pl.BlockSpec(memory_space=pl.ANY))
            aliases[n_in + k] = k
    return n_in, aliases


def _mm_kernel(*refs, nk, glu, has_bias, epilogue, lhs_silu, row_out, heads_hd, heads_scale):
    it = iter(refs)
    x_ref = next(it)
    w_ref = next(it)
    w2_ref = next(it) if glu else None
    b_ref = next(it) if has_bias else None
    b2_ref = next(it) if (has_bias and glu) else None
    o_ref = next(it) if row_out else None
    oh_ref = next(it) if heads_hd else None
    acc_ref = next(it)
    acc2_ref = next(it) if glu else None
    k = pl.program_id(2)

    def partial_sums(first):
        x = x_ref[...]
        if lhs_silu:
            xf = x.astype(F32)
            x = xf * jax.nn.sigmoid(xf)
        xb = x.astype(BF16)
        z = jnp.dot(xb, w_ref[...].astype(BF16), preferred_element_type=F32)
        z2 = jnp.dot(xb, w2_ref[...].astype(BF16), preferred_element_type=F32) if glu else None
        if not first:
            z = acc_ref[...] + z
            if glu:
                z2 = acc2_ref[...] + z2
        return z, z2

    def keep(z, z2):
        acc_ref[...] = z
        if glu:
            acc2_ref[...] = z2

    def finish(z, z2):
        if has_bias:
            z = z + b_ref[...]
        if glu:
            if has_bias:
                z2 = z2 + b2_ref[...]
            z = z * jax.nn.sigmoid(z2)
        if epilogue == "logsigmoid":
            z = jax.nn.log_sigmoid(z)
        if row_out:
            o_ref[...] = z.astype(o_ref.dtype)
        if heads_hd:
            for hh in range(z.shape[1] // heads_hd):
                zh = z[:, hh * heads_hd:(hh + 1) * heads_hd]
                oh_ref[hh] = (zh * heads_scale if heads_scale != 1.0 else zh).astype(oh_ref.dtype)

    if nk == 1:
        finish(*partial_sums(True))
        return

    @pl.when(k == 0)
    def _():
        keep(*partial_sums(True))

    if nk > 2:
        @pl.when((k > 0) & (k < nk - 1))
        def _():
            keep(*partial_sums(False))

    @pl.when(k == nk - 1)
    def _():
        finish(*partial_sums(False))


def _matmul(x, w, layer, *, n_out, col_off=0, bias=None, glu=False, epilogue=None,
            lhs_silu=False, out_dtype=F32, row_out=True, heads_hd=None, heads_scale=1.0, name="mm"):
    m, kdim = x.shape
    nw = w.shape[-1]
    if m <= 64:
        tm = m
        tn = _pick(n_out, (2048, 1024, 512, 256, 128))
        tk = _pick(kdim, (1024, 512, 256, 128))
    else:
        tm = _pick(m, (1408, 1024, 768, 512, 256, 128, 64, 32, 16, 8))
        tn = _pick(n_out, (512, 256, 128) if glu else (1024, 512, 256, 128))
        tk = _pick(kdim, (1024, 512, 256, 128))
    assert col_off % tn == 0 and n_out % tn == 0
    nk = kdim // tk
    coff = col_off // tn
    goff = (col_off + n_out) // tn
    in_specs = [pl.BlockSpec((tm, tk), lambda i, j, k: (i, k)),
                pl.BlockSpec((None, tk, tn), lambda i, j, k: (layer, k, j + coff))]
    args = [x, w]
    if glu:
        in_specs.append(pl.BlockSpec((None, tk, tn), lambda i, j, k: (layer, k, j + goff)))
        args.append(w)
    if bias is not None:
        b3 = bias.reshape(bias.shape[0], 1, nw)
        in_specs.append(pl.BlockSpec((None, 1, tn), lambda i, j, k: (layer, 0, j + coff)))
        args.append(b3)
        if glu:
            in_specs.append(pl.BlockSpec((None, 1, tn), lambda i, j, k: (layer, 0, j + goff)))
            args.append(b3)
    out_shape, out_specs = [], []
    if row_out:
        out_shape.append(jax.ShapeDtypeStruct((m, n_out), out_dtype))
        out_specs.append(pl.BlockSpec((tm, tn), lambda i, j, k: (i, j)))
    if heads_hd:
        assert tn % heads_hd == 0
        out_shape.append(jax.ShapeDtypeStruct((n_out // heads_hd, m, heads_hd), BF16))
        out_specs.append(pl.BlockSpec((tn // heads_hd, tm, heads_hd), lambda i, j, k: (j, i, 0)))
    scratch = [pltpu.VMEM((tm, tn), F32)]
    if glu:
        scratch.append(pltpu.VMEM((tm, tn), F32))
    kern = functools.partial(_mm_kernel, nk=nk, glu=glu, has_bias=bias is not None, epilogue=epilogue,
                             lhs_silu=lhs_silu, row_out=row_out, heads_hd=heads_hd, heads_scale=heads_scale)
    return pl.pallas_call(
        kern,
        out_shape=out_shape,
        grid=(m // tm, n_out // tn, nk),
        in_specs=in_specs,
        out_specs=out_specs,
        scratch_shapes=scratch,
        compiler_params=_cparams(("parallel", "parallel", "arbitrary")),
        name=name,
    )(*args)


def _expand_mod(ref, nb, rpm):
    v = ref[...]
    d = v.shape[-1]
    if nb == 1:
        return v.reshape(1, d)
    return jnp.broadcast_to(v, (nb, rpm, d)).reshape(nb * rpm, d)


def _route(logits):
    tm = logits.shape[0]
    lane = lax.broadcasted_iota(jnp.int32, (tm, LANES), 1)
    neg = jnp.float32(-jnp.inf)
    big = jnp.int32(LANES)
    glog = jnp.where(lane < 8, logits, neg)
    gmax = jnp.max(glog, axis=1, keepdims=True)
    gidx = jnp.min(jnp.where(glog == gmax, lane, big), axis=1, keepdims=True)
    gsum = jnp.sum(jnp.exp(glog - gmax), axis=1, keepdims=True)
    g_w = 1.0 / gsum
    emask = (lane >= 8) & (lane < 72) & (((lane - 8) >> 3) == gidx)
    elog = jnp.where(emask, logits, neg)
    emax = jnp.max(elog, axis=1, keepdims=True)
    i1 = jnp.min(jnp.where(elog == emax, lane, big), axis=1, keepdims=True)
    elog2 = jnp.where(lane == i1, neg, elog)
    emax2 = jnp.max(elog2, axis=1, keepdims=True)
    i2 = jnp.min(jnp.where(elog2 == emax2, lane, big), axis=1, keepdims=True)
    esum = jnp.sum(jnp.exp(elog - emax), axis=1, keepdims=True)
    p1 = 1.0 / esum
    p2 = jnp.exp(emax2 - emax) / esum
    psum = p1 + p2
    gate0 = g_w * p1 / psum
    gate1 = g_w * p2 / psum
    return jnp.where(lane == 0, (i1 - 8).astype(F32),
                     jnp.where(lane == 1, (i2 - 8).astype(F32),
                               jnp.where(lane == 2, gate0, jnp.where(lane == 3, gate1, 0.0))))


def _ln_kernel(*refs, alpha, nb, rpm, first, combine, emit_h, router):
    it = iter(refs)
    x_ref = next(it)
    if not first:
        o_ref = next(it)
        if combine:
            o2_ref = next(it)
            rt_in_ref = next(it)
        g_ref = next(it)
        lg_ref = next(it)
        lb_ref = next(it)
    if emit_h:
        sc_ref = next(it)
        sh_ref = next(it)
    if router:
        whi_ref = next(it)
        wlo_ref = next(it)
        br_ref = next(it)
    xo_ref = None if first else next(it)
    h_ref = next(it) if emit_h else None
    rt_ref = next(it) if router else None

    x = x_ref[...].astype(F32)
    if not first:
        o = o_ref[...].astype(F32)
        if combine:
            rt = rt_in_ref[...]
            o = rt[:, 2:3] * o + rt[:, 3:4] * o2_ref[...].astype(F32)
        g = _expand_mod(g_ref, nb, rpm)
        v = alpha * x + (1.0 + g) * o
        mu = jnp.mean(v, axis=-1, keepdims=True)
        vc = v - mu
        var = jnp.mean(vc * vc, axis=-1, keepdims=True)
        x = vc * lax.rsqrt(var + LN_EPS) * lg_ref[...].reshape(1, -1) + lb_ref[...].reshape(1, -1)
        xo_ref[...] = x
    if emit_h:
        sc = _expand_mod(sc_ref, nb, rpm)
        sh = _expand_mod(sh_ref, nb, rpm)
        h = x * (1.0 + sc) + sh
        h_ref[...] = h.astype(h_ref.dtype)
        if router:
            h_hi = h.astype(BF16)
            h_lo = (h - h_hi.astype(F32)).astype(BF16)
            whi = whi_ref[...]
            logits = (jnp.dot(h_hi, whi, preferred_element_type=F32)
                      + jnp.dot(h_lo, whi, preferred_element_type=F32)
                      + jnp.dot(h_hi, wlo_ref[...], preferred_element_type=F32)
                      + br_ref[...])
            rt_ref[...] = _route(logits)


def _ln_call(x, mods, *, row_off, rows, rpm, seq0, alpha=1.0, o=None, o2=None, o2_row_off=0, route_in=None,
             res_mod=None, ln_g=None, ln_b=None, ln_idx=None, h_mods=None,
             h_dtype=None, router_w=None, prev=None, name="ln"):
    rtot, d = x.shape
    first = o is None
    combine = o2 is not None
    emit_h = h_dtype is not None
    router = router_w is not None
    tm = _pick(rows, (128, 64, 32, 16, 8))
    if tm > rpm:
        assert tm % rpm == 0 and rpm % SUBLANES == 0
        nb = tm // rpm
        assert seq0 % nb == 0
    else:
        assert rpm % tm == 0
        nb = 1
    assert row_off % tm == 0 and rows % tm == 0
    boff = row_off // tm

    def row_map(i):
        return (i + boff, 0)

    def mod_spec(lyr, chunk):
        if nb == 1:
            return pl.BlockSpec((None, 1, 1, d), lambda i: (lyr, seq0 + (i * tm) // rpm, 0, chunk))
        return pl.BlockSpec((None, nb, 1, d), lambda i: (lyr, seq0 // nb + i, 0, chunk))

    row_spec = pl.BlockSpec((tm, d), row_map)
    in_specs = [row_spec]
    args = [x]
    if not first:
        in_specs.append(row_spec)
        args.append(o)
        if combine:
            assert o2_row_off % tm == 0
            in_specs += [pl.BlockSpec((tm, d), lambda i: (i + boff + o2_row_off // tm, 0)),
                         pl.BlockSpec((tm, LANES), row_map)]
            args += [o2, route_in]
        in_specs.append(mod_spec(*res_mod))
        args.append(mods)
        in_specs += [pl.BlockSpec((None, None, 1, d), lambda i: (ln_idx[0], ln_idx[1], 0, 0))] * 2
        args += [ln_g, ln_b]
    if emit_h:
        in_specs += [mod_spec(h_mods[0], h_mods[1]), mod_spec(h_mods[0], h_mods[2])]
        args += [mods, mods]
    if router:
        whi, wlo, br = router_w
        in_specs += [pl.BlockSpec((d, LANES), lambda i: (0, 0)),
                     pl.BlockSpec((d, LANES), lambda i: (0, 0)),
                     pl.BlockSpec((1, LANES), lambda i: (0, 0))]
        args += [whi, wlo, br]
    out_shape = []
    out_specs = []
    if not first:
        out_shape.append(jax.ShapeDtypeStruct((rtot, d), F32))
        out_specs.append(row_spec)
    if emit_h:
        out_shape.append(jax.ShapeDtypeStruct((rtot, d), h_dtype))
        out_specs.append(row_spec)
    if router:
        out_shape.append(jax.ShapeDtypeStruct((rtot, LANES), F32))
        out_specs.append(pl.BlockSpec((tm, LANES), row_map))
    n_in, aliases = _with_prev(args, in_specs, prev)
    kern = functools.partial(_ln_kernel, alpha=alpha, nb=nb, rpm=rpm, first=first,
                             combine=combine, emit_h=emit_h, router=router)
    outs = pl.pallas_call(
        _drop_aliased(kern, n_in, len(aliases)),
        out_shape=out_shape,
        grid=(rows // tm,),
        in_specs=in_specs,
        out_specs=out_specs,
        input_output_aliases=aliases,
        compiler_params=_cparams(("parallel",)),
        name=name,
    )(*args)
    return list(outs)


def _ln_both(x, mods, geom, **kw):
    outs = _ln_call(x, mods, row_off=0, rows=geom["rp"], rpm=geom["seq"], seq0=geom["nbs"],
                    name="ln_prompt", **kw)
    return _ln_call(x, mods, row_off=geom["rp"], rows=geom["rs"], rpm=geom["dseq"], seq0=0,
                    prev=outs, name="ln_sample", **kw)


S5_GC = 16
S5_P = 64
S5_GPB = LANES // S5_GC
S5_CB = S5_GPB * S5_P // LANES


def _s5_kernel(u_ref, win_ref, wout_ref, are_ref, aim_ref, d_ref, h0r_ref, h0i_ref,
               y_ref, hr_ref, hi_ref, bre, bim, sre, sim, *, tl, pitch, nkb):
    t = pl.program_id(2)
    ncb = nkb * S5_CB
    half = S5_CB * LANES

    @pl.when(t == 0)
    def _():
        sre[...] = h0r_ref[...]
        sim[...] = h0i_ref[...]

    u = u_ref[...]
    ub = u.astype(BF16)
    for kk in range(nkb):
        bu = jnp.dot(ub[:, kk * LANES:(kk + 1) * LANES], win_ref[kk], preferred_element_type=F32)
        for j in range(S5_CB):
            c = kk * S5_CB + j
            bre[pl.ds(c * pitch, tl), :] = bu[:, j * LANES:(j + 1) * LANES]
            bim[pl.ds(c * pitch, tl), :] = bu[:, half + j * LANES:half + (j + 1) * LANES]

    nog = ncb // SUBLANES
    a_r = [are_ref[pl.ds(o * SUBLANES, SUBLANES), :] for o in range(nog)]
    a_i = [aim_ref[pl.ds(o * SUBLANES, SUBLANES), :] for o in range(nog)]
    s0 = tuple(sre[pl.ds(o * SUBLANES, SUBLANES), :] for o in range(nog)) + \
        tuple(sim[pl.ds(o * SUBLANES, SUBLANES), :] for o in range(nog))

    def step(tt, carry):
        new_r, new_i = [], []
        for o in range(nog):
            s_r, s_i = carry[o], carry[nog + o]
            idx = pl.ds(o * SUBLANES * pitch + tt, SUBLANES, stride=pitch)
            n_r = a_r[o] * s_r - a_i[o] * s_i + bre[idx, :]
            n_i = a_r[o] * s_i + a_i[o] * s_r + bim[idx, :]
            bre[idx, :] = n_r
            bim[idx, :] = n_i
            new_r.append(n_r)
            new_i.append(n_i)
        return tuple(new_r) + tuple(new_i)

    fin = lax.fori_loop(0, tl, step, s0)
    for o in range(nog):
        sre[pl.ds(o * SUBLANES, SUBLANES), :] = fin[o]
        sim[pl.ds(o * SUBLANES, SUBLANES), :] = fin[nog + o]
    hr_ref[...] = sre[...]
    hi_ref[...] = sim[...]

    ys = []
    for kk in range(nkb):
        parts = [bre[pl.ds((kk * S5_CB + j) * pitch, tl), :] for j in range(S5_CB)]
        parts += [bim[pl.ds((kk * S5_CB + j) * pitch, tl), :] for j in range(S5_CB)]
        lhs = jnp.concatenate(parts, axis=1).astype(BF16)
        ys.append(jnp.dot(lhs, wout_ref[kk], preferred_element_type=F32))
    y = jnp.concatenate(ys, axis=1) if nkb > 1 else ys[0]
    yy = y + d_ref[...] * u
    y_ref[...] = jax.nn.gelu(yy).astype(y_ref.dtype)


def _s5_weights(lam_re, lam_im, log_dt, b_re, b_im, c_re, c_im):
    g, p, gc = b_re.shape
    assert gc == S5_GC and p == S5_P
    dt = jnp.exp(log_dt.astype(F32))[:, None]
    mag = jnp.exp(lam_re * dt)
    ab_re = mag * jnp.cos(lam_im * dt)
    ab_im = mag * jnp.sin(lam_im * dt)
    den = lam_re * lam_re + lam_im * lam_im
    nr = ab_re - 1.0
    z_re = (nr * lam_re + ab_im * lam_im) / den
    z_im = (ab_im * lam_re - nr * lam_im) / den
    bb_re = z_re[..., None] * b_re - z_im[..., None] * b_im
    bb_im = z_re[..., None] * b_im + z_im[..., None] * b_re
    nkb = g // S5_GPB
    eye = jnp.eye(S5_GPB, dtype=F32)

    def in_blocks(bb):
        t = bb.reshape(nkb, S5_GPB, p, gc).transpose(0, 1, 3, 2)
        w = t[:, :, :, None, :] * eye[None, :, None, :, None]
        return w.reshape(nkb, S5_GPB * gc, S5_GPB * p)

    def out_blocks(cc):
        t = cc.reshape(nkb, S5_GPB, gc, p).transpose(0, 1, 3, 2)
        w = t[:, :, :, None, :] * eye[None, :, None, :, None]
        return w.reshape(nkb, S5_GPB * p, S5_GPB * gc)

    w_in = jnp.concatenate([in_blocks(bb_re), in_blocks(bb_im)], axis=2).astype(BF16)
    w_out = jnp.concatenate([out_blocks(c_re.astype(F32)), -out_blocks(c_im.astype(F32))], axis=1).astype(BF16)
    ncols = g * p // LANES
    return w_in, w_out, ab_re.reshape(ncols, LANES), ab_im.reshape(ncols, LANES)


def _s5_call(h, wts, d_skip, h0_re, h0_im, *, row_off, nseq, t_len, prev=None, name="s5"):
    rtot, d = h.shape
    w_in, w_out, a_re, a_im = wts
    ck = _pick(d, (1024, 512, 256))
    nkb = ck // LANES
    ncb = nkb * S5_CB
    assert ncb % SUBLANES == 0
    nkg = d // ck
    tl = _pick(t_len, (256, 128, 64, 32, 16, 8))
    pitch = tl + SUBLANES if (tl // SUBLANES) % 2 == 0 else tl + 2 * SUBLANES
    nt = t_len // tl
    assert row_off % tl == 0
    boff = row_off // tl
    kern = functools.partial(_s5_kernel, tl=tl, pitch=pitch, nkb=nkb)
    in_specs = [
        pl.BlockSpec((tl, ck), lambda kg, b, t: (boff + b * nt + t, kg)),
        pl.BlockSpec((nkb, LANES, 2 * S5_CB * LANES), lambda kg, b, t: (kg, 0, 0)),
        pl.BlockSpec((nkb, 2 * S5_CB * LANES, LANES), lambda kg, b, t: (kg, 0, 0)),
        pl.BlockSpec((ncb, LANES), lambda kg, b, t: (kg, 0)),
        pl.BlockSpec((ncb, LANES), lambda kg, b, t: (kg, 0)),
        pl.BlockSpec((1, ck), lambda kg, b, t: (0, kg)),
        pl.BlockSpec((None, ncb, LANES), lambda kg, b, t: (b, kg, 0)),
        pl.BlockSpec((None, ncb, LANES), lambda kg, b, t: (b, kg, 0)),
    ]
    args = [h, w_in, w_out, a_re, a_im, d_skip, h0_re, h0_im]
    n_in, aliases = _with_prev(args, in_specs, prev)
    nstate = a_re.shape[0]
    out_shape = [jax.ShapeDtypeStruct((rtot, d), BF16),
                 jax.ShapeDtypeStruct((nseq, nstate, LANES), F32),
                 jax.ShapeDtypeStruct((nseq, nstate, LANES), F32)]
    out_specs = [pl.BlockSpec((tl, ck), lambda kg, b, t: (boff + b * nt + t, kg)),
                 pl.BlockSpec((None, ncb, LANES), lambda kg, b, t: (b, kg, 0)),
                 pl.BlockSpec((None, ncb, LANES), lambda kg, b, t: (b, kg, 0))]
    return pl.pallas_call(
        _drop_aliased(kern, n_in, len(aliases)),
        out_shape=out_shape,
        grid=(nkg, nseq, nt),
        in_specs=in_specs,
        out_specs=out_specs,
        scratch_shapes=[pltpu.VMEM((ncb * pitch, LANES), F32), pltpu.VMEM((ncb * pitch, LANES), F32),
                        pltpu.VMEM((ncb, LANES), F32), pltpu.VMEM((ncb, LANES), F32)],
        input_output_aliases=aliases,
        compiler_params=_cparams(("arbitrary", "arbitrary", "arbitrary")),
        name=name,
    )(*args)


def _cumsum_kernel(x_ref, o_ref, carry, *, tl):
    t = pl.program_id(1)

    @pl.when(t == 0)
    def _():
        carry[...] = jnp.zeros_like(carry)

    x = x_ref[...]
    row = lax.broadcasted_iota(jnp.int32, (tl, tl), 0)
    col = lax.broadcasted_iota(jnp.int32, (tl, tl), 1)
    tri = (col <= row).astype(BF16)
    x_hi = x.astype(BF16)
    r1 = x - x_hi.astype(F32)
    x_mid = r1.astype(BF16)
    x_lo = (r1 - x_mid.astype(F32)).astype(BF16)
    c = (jnp.dot(tri, x_hi, preferred_element_type=F32)
         + jnp.dot(tri, x_mid, preferred_element_type=F32)
         + jnp.dot(tri, x_lo, preferred_element_type=F32)) + carry[...]
    o_ref[...] = c
    carry[...] = c[tl - 1:tl, :]


def _cumsum_time(x):
    b, t, hh = x.shape
    tl = _pick(t, (512, 256, 128)) if t % 128 == 0 else t
    return pl.pallas_call(
        functools.partial(_cumsum_kernel, tl=tl),
        out_shape=jax.ShapeDtypeStruct((b, t, hh), F32),
        grid=(b, t // tl),
        in_specs=[pl.BlockSpec((None, tl, hh), lambda i, j: (i, j, 0))],
        out_specs=pl.BlockSpec((None, tl, hh), lambda i, j: (i, j, 0)),
        scratch_shapes=[pltpu.VMEM((1, hh), F32)],
        compiler_params=_cparams(("arbitrary", "arbitrary")),
        name="cumsum_time",
    )(x)


def _flash_kernel(qt_ref, kt_ref, q_ref, k_ref, v_ref, ck_ref, o_ref, m_s, l_s, acc_s, *, tq, rc):
    p = pl.program_id(1)
    qi = qt_ref[p]
    kj = kt_ref[p]

    @pl.when(kj == 0)
    def _():
        m_s[...] = jnp.full_like(m_s, -jnp.inf)
        l_s[...] = jnp.zeros_like(l_s)
        acc_s[...] = jnp.zeros_like(acc_s)

    def update(rows, diagonal):
        s = (lax.dot_general(q_ref[rows, :], k_ref[...], (((1,), (1,)), ((), ())), preferred_element_type=F32)
             - ck_ref[...] * LOG2E)
        if diagonal:
            row = lax.broadcasted_iota(jnp.int32, s.shape, 0) + rows.start
            col = lax.broadcasted_iota(jnp.int32, s.shape, 1)
            s = jnp.where(col <= row, s, -jnp.inf)
        m_prev = m_s[rows, :]
        m_new = jnp.maximum(m_prev, jnp.max(s, axis=1, keepdims=True))
        a = jnp.exp2(m_prev - m_new)
        pexp = jnp.exp2(s - m_new)
        l_s[rows, :] = a * l_s[rows, :] + jnp.sum(pexp, axis=1, keepdims=True)
        acc_s[rows, :] = a * acc_s[rows, :] + jnp.dot(pexp.astype(BF16), v_ref[...], preferred_element_type=F32)
        m_s[rows, :] = m_new

    chunks = [slice(c, c + rc) for c in range(0, tq, rc)]

    @pl.when(kj < qi)
    def _():
        for rows in chunks:
            update(rows, False)

    @pl.when(kj == qi)
    def _():
        for rows in chunks:
            update(rows, True)
        o_ref[...] = (acc_s[...] / l_s[...]).astype(o_ref.dtype)


def _flash_prefill(q, k, v, cum_t, *, t_len):
    heads, rtot, hd = q.shape
    d = heads * hd
    tq = _pick(t_len, (1024, 512, 256, 128))
    nq = t_len // tq
    pairs = [(i, j) for i in range(nq) for j in range(i + 1)]
    qt = jnp.asarray([a for a, _ in pairs], jnp.int32)
    kt = jnp.asarray([b for _, b in pairs], jnp.int32)
    grid_spec = pltpu.PrefetchScalarGridSpec(
        num_scalar_prefetch=2,
        grid=(heads, len(pairs)),
        in_specs=[pl.BlockSpec((None, tq, hd), lambda h, p, qt, kt: (h, qt[p], 0)),
                  pl.BlockSpec((None, tq, hd), lambda h, p, qt, kt: (h, kt[p], 0)),
                  pl.BlockSpec((None, tq, hd), lambda h, p, qt, kt: (h, kt[p], 0)),
                  pl.BlockSpec((None, 1, tq), lambda h, p, qt, kt: (h, 0, kt[p]))],
        out_specs=pl.BlockSpec((tq, hd), lambda h, p, qt, kt: (qt[p], h)),
        scratch_shapes=[pltpu.VMEM((tq, 1), F32), pltpu.VMEM((tq, 1), F32), pltpu.VMEM((tq, hd), F32)],
    )
    return pl.pallas_call(
        functools.partial(_flash_kernel, tq=tq, rc=_pick(tq, (256, 128))),
        out_shape=jax.ShapeDtypeStruct((rtot, d), BF16),
        grid_spec=grid_spec,
        compiler_params=_cparams(("arbitrary", "arbitrary")),
        name="fox_prefill",
    )(qt, kt, q, k, v, cum_t)


def _decode_kernel(q_ref, kn_ref, vn_ref, kc_ref, vc_ref, ck_ref, o_ref, *, past, s_len, hpb, hd):
    nt = (((1,), (1,)), ((), ()))
    row = lax.broadcasted_iota(jnp.int32, (s_len, s_len), 0)
    col = lax.broadcasted_iota(jnp.int32, (s_len, s_len), 1)
    for hh in range(hpb):
        cols = slice(hh * hd, (hh + 1) * hd)
        q = q_ref[hh]
        ck = ck_ref[hh] * LOG2E
        kc = kc_ref[:, cols].astype(BF16)
        vc = vc_ref[:, cols].astype(BF16)
        s_c = lax.dot_general(q, kc, nt, preferred_element_type=F32) - ck[:, :past]
        s_n = lax.dot_general(q, kn_ref[hh], nt, preferred_element_type=F32) - ck[:, past:]
        s_n = jnp.where(col <= row, s_n, -jnp.inf)
        m = jnp.maximum(jnp.max(s_c, axis=1, keepdims=True), jnp.max(s_n, axis=1, keepdims=True))
        p_c = jnp.exp2(s_c - m)
        p_n = jnp.exp2(s_n - m)
        l = jnp.sum(p_c, axis=1, keepdims=True) + jnp.sum(p_n, axis=1, keepdims=True)
        acc = (jnp.dot(p_c.astype(BF16), vc, preferred_element_type=F32)
               + jnp.dot(p_n.astype(BF16), vn_ref[hh], preferred_element_type=F32))
        o_ref[:, cols] = (acc / l).astype(o_ref.dtype)


def _fox_decode(q, k, v, k_cache, v_cache, cum_t, *, row_off, nseq, s_len, layer, prev):
    heads, rtot, hd = q.shape
    d = heads * hd
    past = k_cache.shape[2]
    assert row_off % s_len == 0 and past % LANES == 0
    boff = row_off // s_len
    hpb = _pick(heads, (4, 2, 1))
    new_spec = pl.BlockSpec((hpb, s_len, hd), lambda b, h: (h, boff + b, 0))
    out_spec = pl.BlockSpec((s_len, hpb * hd), lambda b, h: (boff + b, h))
    cache_spec = pl.BlockSpec((None, None, past, hpb * hd), lambda b, h: (layer, b, 0, h))
    in_specs = [new_spec, new_spec, new_spec, cache_spec, cache_spec,
                pl.BlockSpec((None, hpb, 1, past + s_len), lambda b, h: (b, h, 0, 0))]
    args = [q, k, v, k_cache, v_cache, cum_t]
    n_in, aliases = _with_prev(args, in_specs, prev)
    kern = functools.partial(_decode_kernel, past=past, s_len=s_len, hpb=hpb, hd=hd)
    return pl.pallas_call(
        _drop_aliased(kern, n_in, len(aliases)),
        out_shape=[jax.ShapeDtypeStruct((rtot, d), BF16)],
        grid=(nseq, heads // hpb),
        in_specs=in_specs,
        out_specs=[out_spec],
        input_output_aliases=aliases,
        compiler_params=_cparams(("parallel", "parallel")),
        name="fox_decode",
    )(*args)[0]


def _conv_kernel(gb_ref, gc_ref, hin_ref, cw_ref, buf_ref, o_ref, zl_ref, carry, *, tl):
    t = pl.program_id(2)

    @pl.when(t == 0)
    def _():
        carry[...] = buf_ref[...]

    z = gc_ref[...] * hin_ref[...]
    c0 = carry[0:1, :]
    c1 = carry[1:2, :]
    row = lax.broadcasted_iota(jnp.int32, z.shape, 0)
    zm1 = jnp.where(row == 0, c1, pltpu.roll(z, 1, axis=0))
    zm2 = jnp.where(row == 0, c0, jnp.where(row == 1, c1, pltpu.roll(z, 2, axis=0)))
    cw = cw_ref[...]
    acc = cw[0:1, :] * zm2 + cw[1:2, :] * zm1 + cw[2:3, :] * z
    o_ref[...] = (gb_ref[...] * acc).astype(o_ref.dtype)
    last = z[tl - 2:tl, :]
    carry[...] = last
    zl_ref[...] = last


def _conv_call(proj, conv_w, layer, buf, *, row_off, nseq, t_len, d, prev=None, name="conv"):
    rtot = proj.shape[0]
    assert conv_w.shape[1] == 3 and t_len >= 2
    tn = _pick(d, (1024, 512, 256, 128))
    tl = _pick(t_len, (256, 128, 64, 32, 16, 8))
    nt = t_len // tl
    nd = d // tn
    assert row_off % tl == 0
    boff = row_off // tl

    def pspec(part):
        return pl.BlockSpec((tl, tn), lambda b, j, t: (boff + b * nt + t, part * nd + j))

    in_specs = [pspec(0), pspec(1), pspec(2),
                pl.BlockSpec((None, 3, tn), lambda b, j, t: (layer, 0, j)),
                pl.BlockSpec((None, 2, tn), lambda b, j, t: (b, 0, j))]
    args = [proj, proj, proj, conv_w, buf]
    n_in, aliases = _with_prev(args, in_specs, prev)
    return pl.pallas_call(
        _drop_aliased(functools.partial(_conv_kernel, tl=tl), n_in, len(aliases)),
        out_shape=[jax.ShapeDtypeStruct((rtot, d), BF16), jax.ShapeDtypeStruct((nseq, 2, d), F32)],
        grid=(nseq, nd, nt),
        in_specs=in_specs,
        out_specs=[pl.BlockSpec((tl, tn), lambda b, j, t: (boff + b * nt + t, j)),
                   pl.BlockSpec((None, 2, tn), lambda b, j, t: (b, 0, j))],
        scratch_shapes=[pltpu.VMEM((2, tn), F32)],
        input_output_aliases=aliases,
        compiler_params=_cparams(("arbitrary", "arbitrary", "arbitrary")),
        name=name,
    )(*args)


def _gather_kernel(idx_ref, src_ref, o_ref, sem, *, tg):
    base = pl.program_id(0) * tg

    def row_copy(r, row):
        return pltpu.make_async_copy(src_ref.at[pl.ds(row, 1)], o_ref.at[pl.ds(r, 1)], sem)

    def start(r, c):
        row_copy(r, idx_ref[base + r]).start()
        return c

    lax.fori_loop(0, tg, start, 0, unroll=8)
    pltpu.make_async_copy(src_ref.at[pl.ds(0, tg)], o_ref, sem).wait()


def _gather_rows(src, idx, name="gather_rows"):
    m = idx.shape[0]
    d = src.shape[1]
    tg = _pick(m, (256, 128, 64, 32, 16, 8))
    grid_spec = pltpu.PrefetchScalarGridSpec(
        num_scalar_prefetch=1,
        grid=(m // tg,),
        in_specs=[pl.BlockSpec(memory_space=pl.ANY)],
        out_specs=pl.BlockSpec((tg, d), lambda i, idx: (i, 0)),
        scratch_shapes=[pltpu.SemaphoreType.DMA],
    )
    return pl.pallas_call(
        functools.partial(_gather_kernel, tg=tg),
        out_shape=jax.ShapeDtypeStruct((m, d), src.dtype),
        grid_spec=grid_spec,
        compiler_params=_cparams(("arbitrary",)),
        name=name,
    )(idx, src)


WEIGHT_LOOKAHEAD = 2
WEIGHT_RING = WEIGHT_LOOKAHEAD + 1


def _expert_kernel(be_ref, nu_ref, ss_ref, nr_ref, tok_ref, h_ref, wg_ref, wu_ref, wd_ref, o_ref,
                   xbuf, sem, gu_buf, gu_sem, d_buf, d_sem, hg, hu, hb, *, layer, nkt, nnt, tk, tn):
    b = pl.program_id(0)
    j = pl.program_id(1)
    nu = nu_ref[0]
    slot = lax.rem(b, 2)
    nsteps = nkt + nnt

    def row_copy(s, r, tok):
        return pltpu.make_async_copy(h_ref.at[pl.ds(tok, 1)], xbuf.at[s, pl.ds(r, 1)], sem.at[s])

    def start_block(bb, s):
        base = ss_ref[bb]

        def body(r, c):
            row_copy(s, r, tok_ref[base + r]).start()
            return c
        lax.fori_loop(0, nr_ref[bb], body, 0)

    def wait_block(bb, s):
        n = nr_ref[bb]

        def body8(r, c):
            pltpu.make_async_copy(h_ref.at[pl.ds(0, SUBLANES)], xbuf.at[s, pl.ds(0, SUBLANES)], sem.at[s]).wait()
            return c

        def body1(r, c):
            row_copy(s, 0, 0).wait()
            return c
        lax.fori_loop(0, lax.shift_right_logical(n, 3), body8, 0)
        lax.fori_loop(0, lax.bitwise_and(n, SUBLANES - 1), body1, 0)

    def gu_copies(e, jt, ring_slot):
        k0 = pl.multiple_of(jt * tk, tk)
        return (pltpu.make_async_copy(wg_ref.at[layer, e, pl.ds(k0, tk), :], gu_buf.at[ring_slot, 0],
                                      gu_sem.at[ring_slot]),
                pltpu.make_async_copy(wu_ref.at[layer, e, pl.ds(k0, tk), :], gu_buf.at[ring_slot, 1],
                                      gu_sem.at[ring_slot]))

    def d_copy(e, nt, ring_slot):
        n0 = pl.multiple_of(nt * tn, tn)
        return pltpu.make_async_copy(wd_ref.at[layer, e, :, pl.ds(n0, tn)], d_buf.at[ring_slot],
                                     d_sem.at[ring_slot])

    def start_tiles(step):
        b2 = step // nsteps
        j2 = step - b2 * nsteps

        @pl.when(b2 < nu)
        def _():
            e2 = be_ref[b2]

            @pl.when(j2 < nkt)
            def _():
                for c in gu_copies(e2, j2, lax.rem(b2 * nkt + j2, WEIGHT_RING)):
                    c.start(priority=1)

            @pl.when(j2 >= nkt)
            def _():
                d_copy(e2, j2 - nkt, lax.rem(b2 * nnt + j2 - nkt, WEIGHT_RING)).start(priority=1)

    step = b * nsteps + j

    @pl.when(step == 0)
    def _():
        xbuf[...] = jnp.zeros_like(xbuf)
        start_block(0, 0)
        for ahead in range(WEIGHT_LOOKAHEAD):
            start_tiles(ahead)

    @pl.when(b < nu)
    def _():
        start_tiles(step + WEIGHT_LOOKAHEAD)

    @pl.when((b < nu) & (j == 0))
    def _():
        wait_block(b, slot)

        @pl.when(b + 1 < nu)
        def _():
            start_block(b + 1, 1 - slot)

    @pl.when(b < nu)
    def _():
        for jj in range(nkt):
            @pl.when(j == jj)
            def _(jj=jj):
                ring_slot = lax.rem(b * nkt + jj, WEIGHT_RING)
                for c in gu_copies(0, 0, ring_slot):
                    c.wait()
                xb = xbuf[slot, :, jj * tk:(jj + 1) * tk].astype(BF16)
                pg = jnp.dot(xb, gu_buf[ring_slot, 0].astype(BF16), preferred_element_type=F32)
                pu = jnp.dot(xb, gu_buf[ring_slot, 1].astype(BF16), preferred_element_type=F32)
                if jj > 0:
                    pg = hg[...] + pg
                    pu = hu[...] + pu
                if jj < nkt - 1:
                    hg[...] = pg
                    hu[...] = pu
                else:
                    hb[...] = (pg * jax.nn.sigmoid(pg) * pu).astype(BF16)

        @pl.when(j >= nkt)
        def _():
            ring_slot = lax.rem(b * nnt + j - nkt, WEIGHT_RING)
            d_copy(0, 0, ring_slot).wait()
            o_ref[...] = jnp.dot(hb[...], d_buf[ring_slot].astype(BF16), preferred_element_type=F32)


def _expert_call(h, plan, w_gate, w_up, w_down, layer, *, tm):
    block_expert, n_used, src_start, nrows, sorted_tok = plan
    d = h.shape[1]
    de = w_gate.shape[-1]
    nb = block_expert.shape[0]
    rows = nb * tm
    tk = _pick(d, (1024, 512, 256, 128))
    tn = tk
    nkt = d // tk
    nnt = d // tn
    last_n = nnt - 1

    def blk(b, nu):
        return jnp.minimum(b, nu[0] - 1)

    def nidx(b, j, nu):
        return jnp.where(b < nu[0], jnp.maximum(j - nkt, 0), last_n)

    any_spec = pl.BlockSpec(memory_space=pl.ANY)
    grid_spec = pltpu.PrefetchScalarGridSpec(
        num_scalar_prefetch=5,
        grid=(nb, nkt + nnt),
        in_specs=[any_spec, any_spec, any_spec, any_spec],
        out_specs=pl.BlockSpec((tm, tn), lambda b, j, be, nu, ss, nr, tok: (blk(b, nu), nidx(b, j, nu))),
        scratch_shapes=[pltpu.VMEM((2, tm, d), F32), pltpu.SemaphoreType.DMA((2,)),
                        pltpu.VMEM((WEIGHT_RING, 2, tk, de), F32), pltpu.SemaphoreType.DMA((WEIGHT_RING,)),
                        pltpu.VMEM((WEIGHT_RING, de, tn), F32), pltpu.SemaphoreType.DMA((WEIGHT_RING,)),
                        pltpu.VMEM((tm, de), F32), pltpu.VMEM((tm, de), F32), pltpu.VMEM((tm, de), BF16)],
    )
    return pl.pallas_call(
        functools.partial(_expert_kernel, layer=layer, nkt=nkt, nnt=nnt, tk=tk, tn=tn),
        out_shape=jax.ShapeDtypeStruct((rows, d), F32),
        grid_spec=grid_spec,
        compiler_params=_cparams(("arbitrary", "arbitrary")),
        name="moe_experts",
    )(block_expert, n_used, src_start, nrows, sorted_tok, h, w_gate, w_up, w_down)


def _dispatch(ids, n_experts, tm):
    r, k = ids.shape
    a = r * k
    flat_e = ids.reshape(a)
    order = jnp.argsort(flat_e, stable=True).astype(jnp.int32)
    inv = jnp.argsort(order).astype(jnp.int32)
    onehot = flat_e[:, None] == jnp.arange(n_experts, dtype=jnp.int32)[None, :]
    counts = jnp.sum(onehot, axis=0, dtype=jnp.int32)
    padded = (counts + tm - 1) // tm * tm
    pad_end = jnp.cumsum(padded)
    pad_start = pad_end - padded
    start = jnp.cumsum(counts) - counts
    shift = jnp.sum(jnp.where(onehot, (pad_start - start)[None, :], 0), axis=1, dtype=jnp.int32)
    slot_of_assign = (inv + shift).reshape(r, k)
    nb = -(-a // tm) + n_experts
    blk_row = jnp.arange(nb, dtype=jnp.int32) * tm
    block_expert = jnp.minimum(jnp.sum(pad_end[None, :] <= blk_row[:, None], axis=1, dtype=jnp.int32),
                               n_experts - 1)
    q = blk_row - pad_start[block_expert]
    nrows = jnp.clip(counts[block_expert] - q, 0, tm).astype(jnp.int32)
    src_start = jnp.clip(start[block_expert] + q, 0, a - 1).astype(jnp.int32)
    n_used = (pad_end[-1] // tm).astype(jnp.int32).reshape(1)
    sorted_tok = order // k
    return (block_expert, n_used, src_start, nrows, sorted_tok), slot_of_assign


def _moe(h, route, w_gate, w_up, w_down, layer):
    n_experts = w_gate.shape[1]
    ids = route[:, 0:2].astype(jnp.int32)
    a = ids.size
    tm = _pick(a, (512, 256, 128, 64, 32, 16, 8))
    plan, slot_of_assign = _dispatch(ids, n_experts, tm)
    ys = _expert_call(h, plan, w_gate, w_up, w_down, layer, tm=tm)
    return _gather_rows(ys, slot_of_assign.T.reshape(a), name="moe_gather_y")


def kernel(x_prompt, x_sample, c_prompt, c_sample, state_ssm_re, state_ssm_im, cache_fox_k, cache_fox_v, cache_fox_logf, state_conv, ssm_lam_re, ssm_lam_im, ssm_log_dt, ssm_b_re, ssm_b_im, ssm_c_re, ssm_c_im, ssm_d, ssm_w_glu, ssm_b_glu, fox_w_qkvf, fox_b_f, fox_w_o, conv_w_in, conv_w, conv_w_out, ada_w, ada_b, ln_g, ln_b, moe_w_group, moe_b_group, moe_w_expert, moe_b_expert, moe_w_gate, moe_w_up, moe_w_down):
    bp, seq, d = x_prompt.shape
    nbs, dseq, _ = x_sample.shape
    depth = ada_w.shape[0]
    heads, hd = cache_fox_k.shape[3], cache_fox_k.shape[4]
    past = cache_fox_k.shape[2]
    n_groups = moe_w_group.shape[-1]
    n_experts = moe_w_expert.shape[-1]
    assert bp == 1 and n_groups == 8 and n_experts == 64 and heads * hd == d
    alpha = float((2 * depth) ** 0.25)
    rp, rs = bp * seq, nbs * dseq
    geom = dict(rp=rp, rs=rs, seq=seq, dseq=dseq, nbs=nbs)

    x = jnp.concatenate([x_prompt.reshape(rp, d), x_sample.reshape(rs, d)], axis=0)

    nmod = -(-(nbs + bp) // SUBLANES) * SUBLANES
    c_all = jnp.concatenate([c_sample, c_prompt, jnp.zeros((nmod - nbs - bp, d), F32)], axis=0)
    mods = jnp.stack([_matmul(c_all, ada_w, i, n_out=6 * d, bias=ada_b, lhs_silu=True, name="ada")[0]
                      for i in range(depth)]).reshape(depth, nmod, 1, 6 * d)
    ln_g4 = ln_g.reshape(depth, 2, 1, d)
    ln_b4 = ln_b.reshape(depth, 2, 1, d)
    SH1, SC1, G1, SH2, SC2, G2 = range(6)

    def mixer_h_dtype(i):
        return F32 if i % N_MIXERS == 0 else BF16

    def router_weights(i):
        w = jnp.concatenate([moe_w_group[i], moe_w_expert[i],
                             jnp.zeros((d, LANES - n_groups - n_experts), F32)], axis=1)
        b = jnp.concatenate([moe_b_group[i], moe_b_expert[i],
                             jnp.zeros((LANES - n_groups - n_experts,), F32)]).reshape(1, LANES)
        w_hi = w.astype(BF16)
        w_lo = (w - w_hi.astype(F32)).astype(BF16)
        return w_hi, w_lo, b

    (h,) = _ln_both(x, mods, geom, h_mods=(0, SC1, SH1), h_dtype=mixer_h_dtype(0))

    re_p, im_p, re_s, im_s = [], [], [], []
    k_p, v_p, lf_p, k_s, v_s, lf_s = [], [], [], [], [], []
    conv_p, conv_s = [], []
    for i in range(depth):
        j = i // N_MIXERS
        kind = i % N_MIXERS
        if kind == 0:
            wts = _s5_weights(ssm_lam_re[j].astype(F32), ssm_lam_im[j].astype(F32), ssm_log_dt[j],
                              ssm_b_re[j].astype(F32), ssm_b_im[j].astype(F32), ssm_c_re[j], ssm_c_im[j])
            nstate = wts[2].shape[0]
            zero_state = jnp.zeros((bp, nstate, LANES), F32)
            d_skip = ssm_d[j].reshape(1, d)
            outs = _s5_call(h, wts, d_skip, zero_state, zero_state, row_off=0, nseq=bp, t_len=seq,
                            name="s5_prompt")
            outs2 = _s5_call(h, wts, d_skip, state_ssm_re[j].reshape(nbs, nstate, LANES),
                             state_ssm_im[j].reshape(nbs, nstate, LANES), row_off=rp, nseq=nbs,
                             t_len=dseq, prev=[outs[0]], name="s5_sample")
            gshape = ssm_lam_re.shape[1:]
            re_p.append(outs[1].reshape((bp,) + gshape))
            im_p.append(outs[2].reshape((bp,) + gshape))
            re_s.append(outs2[1].reshape((nbs,) + gshape))
            im_s.append(outs2[2].reshape((nbs,) + gshape))
            (out,) = _matmul(outs2[0], ssm_w_glu, j, n_out=d, bias=ssm_b_glu, glu=True, name="s5_glu")
        elif kind == 1:
            (qh,) = _matmul(h, fox_w_qkvf, j, n_out=d, col_off=0, row_out=False, heads_hd=hd,
                            heads_scale=float(hd) ** -0.5 * LOG2E, name="fox_q")
            k, kh = _matmul(h, fox_w_qkvf, j, n_out=d, col_off=d, heads_hd=hd, name="fox_k")
            v, vh = _matmul(h, fox_w_qkvf, j, n_out=d, col_off=2 * d, heads_hd=hd, name="fox_v")
            w_f = fox_w_qkvf[:, :, 3 * d:]
            (lf,) = _matmul(h, w_f, j, n_out=heads, bias=fox_b_f, epilogue="logsigmoid", name="fox_f")
            lf_prompt = lf[:rp].reshape(bp, seq, heads)
            lf_sample = lf[rp:].reshape(nbs, dseq, heads)
            cum_p = _cumsum_time(lf_prompt)
            cum_s = _cumsum_time(jnp.concatenate([cache_fox_logf[j].astype(F32), lf_sample], axis=1))
            o_p = _flash_prefill(qh, kh, vh, cum_p[0].T.reshape(heads, 1, seq), t_len=seq)
            o = _fox_decode(qh, kh, vh, cache_fox_k.reshape(cache_fox_k.shape[:3] + (d,)),
                            cache_fox_v.reshape(cache_fox_v.shape[:3] + (d,)),
                            cum_s.transpose(0, 2, 1).reshape(nbs, heads, 1, past + dseq),
                            row_off=rp, nseq=nbs, s_len=dseq, layer=j, prev=[o_p])
            (out,) = _matmul(o, fox_w_o, j, n_out=d, name="fox_o")
            k_p.append(k[:rp].reshape(bp, seq, heads, hd))
            v_p.append(v[:rp].reshape(bp, seq, heads, hd))
            lf_p.append(lf_prompt)
            k_s.append(k[rp:].reshape(nbs, dseq, heads, hd))
            v_s.append(v[rp:].reshape(nbs, dseq, heads, hd))
            lf_s.append(lf_sample)
        else:
            (proj,) = _matmul(h, conv_w_in, j, n_out=3 * d, name="conv_in")
            zero_buf = jnp.zeros((bp, conv_w.shape[1] - 1, d), F32)
            outs = _conv_call(proj, conv_w, j, zero_buf, row_off=0, nseq=bp, t_len=seq, d=d, name="conv_prompt")
            outs2 = _conv_call(proj, conv_w, j, state_conv[j].astype(F32), row_off=rp, nseq=nbs, t_len=dseq,
                               d=d, prev=[outs[0]], name="conv_sample")
            conv_p.append(outs[1])
            conv_s.append(outs2[1])
            (out,) = _matmul(outs2[0], conv_w_out, j, n_out=d, name="conv_out")

        x, h2, route = _ln_both(x, mods, geom, alpha=alpha, o=out, res_mod=(i, G1), ln_g=ln_g4, ln_b=ln_b4,
                                ln_idx=(i, 0), h_mods=(i, SC2, SH2), h_dtype=F32,
                                router_w=router_weights(i))
        y01 = _moe(h2, route, moe_w_gate, moe_w_up, moe_w_down, i)
        ffn = dict(alpha=alpha, o=y01, o2=y01, o2_row_off=rp + rs, route_in=route, res_mod=(i, G2),
                   ln_g=ln_g4, ln_b=ln_b4, ln_idx=(i, 1))
        if i + 1 < depth:
            x, h = _ln_both(x, mods, geom, h_mods=(i + 1, SC1, SH1), h_dtype=mixer_h_dtype(i + 1), **ffn)
        else:
            (x,) = _ln_both(x, mods, geom, **ffn)

    y_prompt = x[:rp].reshape(bp, seq, d)
    y_sample = x[rp:].reshape(nbs, dseq, d)
    return (y_prompt, y_sample, jnp.stack(re_p), jnp.stack(im_p), jnp.stack(k_p), jnp.stack(v_p),
            jnp.stack(lf_p), jnp.stack(conv_p), jnp.stack(re_s), jnp.stack(im_s), jnp.stack(k_s),
            jnp.stack(v_s), jnp.stack(lf_s), jnp.stack(conv_s))
```

```python
import functools

import jax
import jax.numpy as jnp
from jax import lax
from jax.experimental import pallas as pl
from jax.experimental.pallas import tpu as pltpu

LN_EPS = 1e-5
N_MIXERS = 3
LOG2E = 1.4426950408889634

V7X_VMEM_BYTES = 64 * 1024 * 1024
VMEM_LIMIT_BYTES = V7X_VMEM_BYTES - 8 * 1024 * 1024
LANES = 128
SUBLANES = 8

F32 = jnp.float32
BF16 = jnp.bfloat16


def _pick(n, candidates):
    for c in candidates:
        if c <= n and n % c == 0:
            return c
    return n


def _cparams(sem):
    return pltpu.CompilerParams(dimension_semantics=sem, vmem_limit_bytes=VMEM_LIMIT_BYTES)


def _drop_aliased(kernel, n_in, n_alias):
    def wrapped(*refs):
        kernel(*(refs[:n_in] + refs[n_in + n_alias:]))
    return wrapped


def _with_prev(args, in_specs, prev):
    aliases = {}
    n_in = len(args)
    if prev is not None:
        for k, p in enumerate(prev):
            args.append(p)
            in_specs.append(pl.BlockSpec(memory_space=pl.ANY))
            aliases[n_in + k] = k
    return n_in, aliases


def _mm_kernel(*refs, nk, glu, has_bias, epilogue, lhs_silu, row_out, heads_hd, heads_scale):
    it = iter(refs)
    x_ref = next(it)
    w_ref = next(it)
    w2_ref = next(it) if glu else None
    b_ref = next(it) if has_bias else None
    b2_ref = next(it) if (has_bias and glu) else None
    o_ref = next(it) if row_out else None
    oh_ref = next(it) if heads_hd else None
    acc_ref = next(it)
    acc2_ref = next(it) if glu else None
    k = pl.program_id(2)

    def partial_sums(first):
        x = x_ref[...]
        if lhs_silu:
            xf = x.astype(F32)
            x = xf * jax.nn.sigmoid(xf)
        xb = x.astype(BF16)
        z = jnp.dot(xb, w_ref[...].astype(BF16), preferred_element_type=F32)
        z2 = jnp.dot(xb, w2_ref[...].astype(BF16), preferred_element_type=F32) if glu else None
        if not first:
            z = acc_ref[...] + z
            if glu:
                z2 = acc2_ref[...] + z2
        return z, z2

    def keep(z, z2):
        acc_ref[...] = z
        if glu:
            acc2_ref[...] = z2

    def finish(z, z2):
        if has_bias:
            z = z + b_ref[...]
        if glu:
            if has_bias:
                z2 = z2 + b2_ref[...]
            z = z * jax.nn.sigmoid(z2)
        if epilogue == "logsigmoid":
            z = jax.nn.log_sigmoid(z)
        if row_out:
            o_ref[...] = z.astype(o_ref.dtype)
        if heads_hd:
            for hh in range(z.shape[1] // heads_hd):
                zh = z[:, hh * heads_hd:(hh + 1) * heads_hd]
                oh_ref[hh] = (zh * heads_scale if heads_scale != 1.0 else zh).astype(oh_ref.dtype)

    if nk == 1:
        finish(*partial_sums(True))
        return

    @pl.when(k == 0)
    def _():
        keep(*partial_sums(True))

    if nk > 2:
        @pl.when((k > 0) & (k < nk - 1))
        def _():
            keep(*partial_sums(False))

    @pl.when(k == nk - 1)
    def _():
        finish(*partial_sums(False))


def _matmul(x, w, layer, *, n_out, col_off=0, bias=None, glu=False, epilogue=None,
            lhs_silu=False, out_dtype=F32, row_out=True, heads_hd=None, heads_scale=1.0, name="mm"):
    m, kdim = x.shape
    nw = w.shape[-1]
    if m <= 64:
        tm = m
        tn = _pick(n_out, (2048, 1024, 512, 256, 128))
        tk = _pick(kdim, (1024, 512, 256, 128))
    else:
        tm = _pick(m, (1408, 1024, 768, 512, 256, 128, 64, 32, 16, 8))
        tn = _pick(n_out, (512, 256, 128) if glu else (1024, 512, 256, 128))
        tk = _pick(kdim, (1024, 512, 256, 128))
    assert col_off % tn == 0 and n_out % tn == 0
    nk = kdim // tk
    coff = col_off // tn
    goff = (col_off + n_out) // tn
    in_specs = [pl.BlockSpec((tm, tk), lambda i, j, k: (i, k)),
                pl.BlockSpec((None, tk, tn), lambda i, j, k: (layer, k, j + coff))]
    args = [x, w]
    if glu:
        in_specs.append(pl.BlockSpec((None, tk, tn), lambda i, j, k: (layer, k, j + goff)))
        args.append(w)
    if bias is not None:
        b3 = bias.reshape(bias.shape[0], 1, nw)
        in_specs.append(pl.BlockSpec((None, 1, tn), lambda i, j, k: (layer, 0, j + coff)))
        args.append(b3)
        if glu:
            in_specs.append(pl.BlockSpec((None, 1, tn), lambda i, j, k: (layer, 0, j + goff)))
            args.append(b3)
    out_shape, out_specs = [], []
    if row_out:
        out_shape.append(jax.ShapeDtypeStruct((m, n_out), out_dtype))
        out_specs.append(pl.BlockSpec((tm, tn), lambda i, j, k: (i, j)))
    if heads_hd:
        assert tn % heads_hd == 0
        out_shape.append(jax.ShapeDtypeStruct((n_out // heads_hd, m, heads_hd), BF16))
        out_specs.append(pl.BlockSpec((tn // heads_hd, tm, heads_hd), lambda i, j, k: (j, i, 0)))
    scratch = [pltpu.VMEM((tm, tn), F32)]
    if glu:
        scratch.append(pltpu.VMEM((tm, tn), F32))
    kern = functools.partial(_mm_kernel, nk=nk, glu=glu, has_bias=bias is not None, epilogue=epilogue,
                             lhs_silu=lhs_silu, row_out=row_out, heads_hd=heads_hd, heads_scale=heads_scale)
    return pl.pallas_call(
        kern,
        out_shape=out_shape,
        grid=(m // tm, n_out // tn, nk),
        in_specs=in_specs,
        out_specs=out_specs,
        scratch_shapes=scratch,
        compiler_params=_cparams(("parallel", "parallel", "arbitrary")),
        name=name,
    )(*args)


def _expand_mod(ref, nb, rpm):
    v = ref[...]
    d = v.shape[-1]
    if nb == 1:
        return v.reshape(1, d)
    return jnp.broadcast_to(v, (nb, rpm, d)).reshape(nb * rpm, d)


def _route(logits):
    tm = logits.shape[0]
    lane = lax.broadcasted_iota(jnp.int32, (tm, LANES), 1)
    neg = jnp.float32(-jnp.inf)
    big = jnp.int32(LANES)
    glog = jnp.where(lane < 8, logits, neg)
    gmax = jnp.max(glog, axis=1, keepdims=True)
    gidx = jnp.min(jnp.where(glog == gmax, lane, big), axis=1, keepdims=True)
    gsum = jnp.sum(jnp.exp(glog - gmax), axis=1, keepdims=True)
    g_w = 1.0 / gsum
    emask = (lane >= 8) & (lane < 72) & (((lane - 8) >> 3) == gidx)
    elog = jnp.where(emask, logits, neg)
    emax = jnp.max(elog, axis=1, keepdims=True)
    i1 = jnp.min(jnp.where(elog == emax, lane, big), axis=1, keepdims=True)
    elog2 = jnp.where(lane == i1, neg, elog)
    emax2 = jnp.max(elog2, axis=1, keepdims=True)
    i2 = jnp.min(jnp.where(elog2 == emax2, lane, big), axis=1, keepdims=True)
    esum = jnp.sum(jnp.exp(elog - emax), axis=1, keepdims=True)
    p1 = 1.0 / esum
    p2 = jnp.exp(emax2 - emax) / esum
    psum = p1 + p2
    gate0 = g_w * p1 / psum
    gate1 = g_w * p2 / psum
    return jnp.where(lane == 0, (i1 - 8).astype(F32),
                     jnp.where(lane == 1, (i2 - 8).astype(F32),
                               jnp.where(lane == 2, gate0, jnp.where(lane == 3, gate1, 0.0))))


def _ln_kernel(*refs, alpha, nb, rpm, first, combine, emit_h, router, tm, boff, rtot):
    it = iter(refs)
    slots_ref = next(it) if combine else None
    x_ref = next(it)
    if not first:
        o_ref = next(it)
        if combine:
            rt_in_ref = next(it)
        g_ref = next(it)
        lg_ref = next(it)
        lb_ref = next(it)
    if emit_h:
        sc_ref = next(it)
        sh_ref = next(it)
    if router:
        whi_ref = next(it)
        wlo_ref = next(it)
        br_ref = next(it)
    xo_ref = None if first else next(it)
    h_ref = next(it) if emit_h else None
    rt_ref = next(it) if router else None
    if combine:
        ybuf = next(it)
        sem = next(it)

    x = x_ref[...].astype(F32)
    if not first:
        if combine:
            i = pl.program_id(0)
            slot = lax.rem(i, 2)

            def start(ii, s):
                base = (ii + boff) * tm

                def body(r, c):
                    for choice in range(2):
                        src_row = slots_ref[choice * rtot + base + r]
                        pltpu.make_async_copy(o_ref.at[pl.ds(src_row, 1)], ybuf.at[s, choice, pl.ds(r, 1)],
                                              sem.at[s]).start()
                    return c
                lax.fori_loop(0, tm, body, 0, unroll=8)

            @pl.when(i == 0)
            def _():
                start(0, 0)

            for choice in range(2):
                pltpu.make_async_copy(o_ref.at[pl.ds(0, tm)], ybuf.at[slot, choice], sem.at[slot]).wait()

            @pl.when(i + 1 < pl.num_programs(0))
            def _():
                start(i + 1, 1 - slot)

            rt = rt_in_ref[...]
            o = rt[:, 2:3] * ybuf[slot, 0] + rt[:, 3:4] * ybuf[slot, 1]
        else:
            o = o_ref[...].astype(F32)
        g = _expand_mod(g_ref, nb, rpm)
        v = alpha * x + (1.0 + g) * o
        mu = jnp.mean(v, axis=-1, keepdims=True)
        vc = v - mu
        var = jnp.mean(vc * vc, axis=-1, keepdims=True)
        x = vc * lax.rsqrt(var + LN_EPS) * lg_ref[...].reshape(1, -1) + lb_ref[...].reshape(1, -1)
        xo_ref[...] = x
    if emit_h:
        sc = _expand_mod(sc_ref, nb, rpm)
        sh = _expand_mod(sh_ref, nb, rpm)
        h = x * (1.0 + sc) + sh
        h_ref[...] = h.astype(h_ref.dtype)
        if router:
            h_hi = h.astype(BF16)
            h_lo = (h - h_hi.astype(F32)).astype(BF16)
            whi = whi_ref[...]
            logits = (jnp.dot(h_hi, whi, preferred_element_type=F32)
                      + jnp.dot(h_lo, whi, preferred_element_type=F32)
                      + jnp.dot(h_hi, wlo_ref[...], preferred_element_type=F32)
                      + br_ref[...])
            rt_ref[...] = _route(logits)


def _ln_call(x, mods, *, rtot, row_off, rows, rpm, seq0, alpha=1.0, o=None, slots=None, route_in=None,
             res_mod=None, ln_g=None, ln_b=None, ln_idx=None, h_mods=None,
             h_dtype=None, router_w=None, prev=None, name="ln"):
    d = x.shape[1]
    first = o is None
    combine = slots is not None
    emit_h = h_dtype is not None
    router = router_w is not None
    tm = _pick(rows, (128, 64, 32, 16, 8))
    if tm > rpm:
        assert tm % rpm == 0 and rpm % SUBLANES == 0
        nb = tm // rpm
        assert seq0 % nb == 0
    else:
        assert rpm % tm == 0
        nb = 1
    assert row_off % tm == 0 and rows % tm == 0
    boff = row_off // tm
    x_boff = boff if x.shape[0] == rtot else 0

    def row_map(i, *_):
        return (i + boff, 0)

    def mod_spec(lyr, chunk):
        if nb == 1:
            return pl.BlockSpec((None, 1, 1, d), lambda i, *_: (lyr, seq0 + (i * tm) // rpm, 0, chunk))
        return pl.BlockSpec((None, nb, 1, d), lambda i, *_: (lyr, seq0 // nb + i, 0, chunk))

    row_spec = pl.BlockSpec((tm, d), row_map)
    in_specs = [pl.BlockSpec((tm, d), lambda i, *_: (i + x_boff, 0))]
    args = [x]
    if not first:
        if combine:
            in_specs += [pl.BlockSpec(memory_space=pl.ANY), pl.BlockSpec((tm, LANES), row_map)]
            args += [o, route_in]
        else:
            in_specs.append(row_spec)
            args.append(o)
        in_specs.append(mod_spec(*res_mod))
        args.append(mods)
        in_specs += [pl.BlockSpec((None, None, 1, d), lambda i, *_: (ln_idx[0], ln_idx[1], 0, 0))] * 2
        args += [ln_g, ln_b]
    if emit_h:
        in_specs += [mod_spec(h_mods[0], h_mods[1]), mod_spec(h_mods[0], h_mods[2])]
        args += [mods, mods]
    if router:
        whi, wlo, br = router_w
        in_specs += [pl.BlockSpec((d, LANES), lambda i, *_: (0, 0)),
                     pl.BlockSpec((d, LANES), lambda i, *_: (0, 0)),
                     pl.BlockSpec((1, LANES), lambda i, *_: (0, 0))]
        args += [whi, wlo, br]
    out_shape = []
    out_specs = []
    if not first:
        out_shape.append(jax.ShapeDtypeStruct((rtot, d), F32))
        out_specs.append(row_spec)
    if emit_h:
        out_shape.append(jax.ShapeDtypeStruct((rtot, d), h_dtype))
        out_specs.append(row_spec)
    if router:
        out_shape.append(jax.ShapeDtypeStruct((rtot, LANES), F32))
        out_specs.append(pl.BlockSpec((tm, LANES), row_map))
    scratch = []
    if combine:
        args = [slots] + args
        scratch = [pltpu.VMEM((2, 2, tm, d), F32), pltpu.SemaphoreType.DMA((2,))]
    n_in, aliases = _with_prev(args, in_specs, prev)
    kern = functools.partial(_ln_kernel, alpha=alpha, nb=nb, rpm=rpm, first=first, combine=combine,
                             emit_h=emit_h, router=router, tm=tm, boff=boff, rtot=rtot)
    grid_spec = pltpu.PrefetchScalarGridSpec(
        num_scalar_prefetch=1 if combine else 0,
        grid=(rows // tm,),
        in_specs=in_specs,
        out_specs=out_specs,
        scratch_shapes=scratch,
    )
    outs = pl.pallas_call(
        _drop_aliased(kern, n_in, len(aliases)),
        out_shape=out_shape,
        grid_spec=grid_spec,
        input_output_aliases=aliases,
        compiler_params=_cparams(("arbitrary" if combine else "parallel",)),
        name=name,
    )(*args)
    return list(outs)


def _ln_both(xs, mods, geom, **kw):
    xp, xsm = xs if isinstance(xs, tuple) else (xs, xs)
    rtot = geom["rp"] + geom["rs"]
    outs = _ln_call(xp, mods, rtot=rtot, row_off=0, rows=geom["rp"], rpm=geom["seq"], seq0=geom["nbs"],
                    name="ln_prompt", **kw)
    return _ln_call(xsm, mods, rtot=rtot, row_off=geom["rp"], rows=geom["rs"], rpm=geom["dseq"], seq0=0,
                    prev=outs, name="ln_sample", **kw)


S5_GC = 16
S5_P = 64
S5_GPB = LANES // S5_GC
S5_CB = S5_GPB * S5_P // LANES


def _s5_kernel(u_ref, win_ref, wout_ref, are_ref, aim_ref, d_ref, h0r_ref, h0i_ref,
               y_ref, hr_ref, hi_ref, bre, bim, sre, sim, *, tl, pitch, nkb):
    t = pl.program_id(2)
    ncb = nkb * S5_CB
    half = S5_CB * LANES

    @pl.when(t == 0)
    def _():
        sre[...] = h0r_ref[...]
        sim[...] = h0i_ref[...]

    u = u_ref[...]
    ub = u.astype(BF16)
    for kk in range(nkb):
        bu = jnp.dot(ub[:, kk * LANES:(kk + 1) * LANES], win_ref[kk], preferred_element_type=F32)
        for j in range(S5_CB):
            c = kk * S5_CB + j
            bre[pl.ds(c * pitch, tl), :] = bu[:, j * LANES:(j + 1) * LANES]
            bim[pl.ds(c * pitch, tl), :] = bu[:, half + j * LANES:half + (j + 1) * LANES]

    nog = ncb // SUBLANES
    a_r = [are_ref[pl.ds(o * SUBLANES, SUBLANES), :] for o in range(nog)]
    a_i = [aim_ref[pl.ds(o * SUBLANES, SUBLANES), :] for o in range(nog)]
    s0 = tuple(sre[pl.ds(o * SUBLANES, SUBLANES), :] for o in range(nog)) + \
        tuple(sim[pl.ds(o * SUBLANES, SUBLANES), :] for o in range(nog))

    def step(tt, carry):
        new_r, new_i = [], []
        for o in range(nog):
            s_r, s_i = carry[o], carry[nog + o]
            idx = pl.ds(o * SUBLANES * pitch + tt, SUBLANES, stride=pitch)
            n_r = a_r[o] * s_r - a_i[o] * s_i + bre[idx, :]
            n_i = a_r[o] * s_i + a_i[o] * s_r + bim[idx, :]
            bre[idx, :] = n_r
            bim[idx, :] = n_i
            new_r.append(n_r)
            new_i.append(n_i)
        return tuple(new_r) + tuple(new_i)

    fin = lax.fori_loop(0, tl, step, s0, unroll=4)
    for o in range(nog):
        sre[pl.ds(o * SUBLANES, SUBLANES), :] = fin[o]
        sim[pl.ds(o * SUBLANES, SUBLANES), :] = fin[nog + o]
    hr_ref[...] = sre[...]
    hi_ref[...] = sim[...]

    ys = []
    for kk in range(nkb):
        parts = [bre[pl.ds((kk * S5_CB + j) * pitch, tl), :] for j in range(S5_CB)]
        parts += [bim[pl.ds((kk * S5_CB + j) * pitch, tl), :] for j in range(S5_CB)]
        lhs = jnp.concatenate(parts, axis=1).astype(BF16)
        ys.append(jnp.dot(lhs, wout_ref[kk], preferred_element_type=F32))
    y = jnp.concatenate(ys, axis=1) if nkb > 1 else ys[0]
    yy = y + d_ref[...] * u
    y_ref[...] = jax.nn.gelu(yy).astype(y_ref.dtype)


def _s5_weights(lam_re, lam_im, log_dt, b_re, b_im, c_re, c_im):
    g, p, gc = b_re.shape
    assert gc == S5_GC and p == S5_P
    dt = jnp.exp(log_dt.astype(F32))[:, None]
    mag = jnp.exp(lam_re * dt)
    ab_re = mag * jnp.cos(lam_im * dt)
    ab_im = mag * jnp.sin(lam_im * dt)
    den = lam_re * lam_re + lam_im * lam_im
    nr = ab_re - 1.0
    z_re = (nr * lam_re + ab_im * lam_im) / den
    z_im = (ab_im * lam_re - nr * lam_im) / den
    bb_re = z_re[..., None] * b_re - z_im[..., None] * b_im
    bb_im = z_re[..., None] * b_im + z_im[..., None] * b_re
    nkb = g // S5_GPB
    eye = jnp.eye(S5_GPB, dtype=F32)

    def in_blocks(bb):
        t = bb.reshape(nkb, S5_GPB, p, gc).transpose(0, 1, 3, 2)
        w = t[:, :, :, None, :] * eye[None, :, None, :, None]
        return w.reshape(nkb, S5_GPB * gc, S5_GPB * p)

    def out_blocks(cc):
        t = cc.reshape(nkb, S5_GPB, gc, p).transpose(0, 1, 3, 2)
        w = t[:, :, :, None, :] * eye[None, :, None, :, None]
        return w.reshape(nkb, S5_GPB * p, S5_GPB * gc)

    w_in = jnp.concatenate([in_blocks(bb_re), in_blocks(bb_im)], axis=2).astype(BF16)
    w_out = jnp.concatenate([out_blocks(c_re.astype(F32)), -out_blocks(c_im.astype(F32))], axis=1).astype(BF16)
    ncols = g * p // LANES
    return w_in, w_out, ab_re.reshape(ncols, LANES), ab_im.reshape(ncols, LANES)


def _s5_call(h, wts, d_skip, h0_re, h0_im, *, row_off, nseq, t_len, prev=None, name="s5"):
    rtot, d = h.shape
    w_in, w_out, a_re, a_im = wts
    ck = _pick(d, (1024, 512, 256))
    nkb = ck // LANES
    ncb = nkb * S5_CB
    assert ncb % SUBLANES == 0
    nkg = d // ck
    tl = _pick(t_len, (256, 128, 64, 32, 16, 8))
    pitch = tl + SUBLANES if (tl // SUBLANES) % 2 == 0 else tl + 2 * SUBLANES
    nt = t_len // tl
    assert row_off % tl == 0
    boff = row_off // tl
    kern = functools.partial(_s5_kernel, tl=tl, pitch=pitch, nkb=nkb)
    in_specs = [
        pl.BlockSpec((tl, ck), lambda kg, b, t: (boff + b * nt + t, kg)),
        pl.BlockSpec((nkb, LANES, 2 * S5_CB * LANES), lambda kg, b, t: (kg, 0, 0)),
        pl.BlockSpec((nkb, 2 * S5_CB * LANES, LANES), lambda kg, b, t: (kg, 0, 0)),
        pl.BlockSpec((ncb, LANES), lambda kg, b, t: (kg, 0)),
        pl.BlockSpec((ncb, LANES), lambda kg, b, t: (kg, 0)),
        pl.BlockSpec((1, ck), lambda kg, b, t: (0, kg)),
        pl.BlockSpec((None, ncb, LANES), lambda kg, b, t: (b, kg, 0)),
        pl.BlockSpec((None, ncb, LANES), lambda kg, b, t: (b, kg, 0)),
    ]
    args = [h, w_in, w_out, a_re, a_im, d_skip, h0_re, h0_im]
    n_in, aliases = _with_prev(args, in_specs, prev)
    nstate = a_re.shape[0]
    out_shape = [jax.ShapeDtypeStruct((rtot, d), BF16),
                 jax.ShapeDtypeStruct((nseq, nstate, LANES), F32),
                 jax.ShapeDtypeStruct((nseq, nstate, LANES), F32)]
    out_specs = [pl.BlockSpec((tl, ck), lambda kg, b, t: (boff + b * nt + t, kg)),
                 pl.BlockSpec((None, ncb, LANES), lambda kg, b, t: (b, kg, 0)),
                 pl.BlockSpec((None, ncb, LANES), lambda kg, b, t: (b, kg, 0))]
    return pl.pallas_call(
        _drop_aliased(kern, n_in, len(aliases)),
        out_shape=out_shape,
        grid=(nkg, nseq, nt),
        in_specs=in_specs,
        out_specs=out_specs,
        scratch_shapes=[pltpu.VMEM((ncb * pitch, LANES), F32), pltpu.VMEM((ncb * pitch, LANES), F32),
                        pltpu.VMEM((ncb, LANES), F32), pltpu.VMEM((ncb, LANES), F32)],
        input_output_aliases=aliases,
        compiler_params=_cparams(("arbitrary", "arbitrary", "arbitrary")),
        name=name,
    )(*args)


def _cumsum_kernel(x_ref, o_ref, carry, *, tl):
    t = pl.program_id(1)

    @pl.when(t == 0)
    def _():
        carry[...] = jnp.zeros_like(carry)

    x = x_ref[...]
    row = lax.broadcasted_iota(jnp.int32, (tl, tl), 0)
    col = lax.broadcasted_iota(jnp.int32, (tl, tl), 1)
    tri = (col <= row).astype(BF16)
    x_hi = x.astype(BF16)
    r1 = x - x_hi.astype(F32)
    x_mid = r1.astype(BF16)
    x_lo = (r1 - x_mid.astype(F32)).astype(BF16)
    c = (jnp.dot(tri, x_hi, preferred_element_type=F32)
         + jnp.dot(tri, x_mid, preferred_element_type=F32)
         + jnp.dot(tri, x_lo, preferred_element_type=F32)) + carry[...]
    o_ref[...] = c
    carry[...] = c[tl - 1:tl, :]


def _cumsum_time(x):
    b, t, hh = x.shape
    tl = _pick(t, (512, 256, 128)) if t % 128 == 0 else t
    return pl.pallas_call(
        functools.partial(_cumsum_kernel, tl=tl),
        out_shape=jax.ShapeDtypeStruct((b, t, hh), F32),
        grid=(b, t // tl),
        in_specs=[pl.BlockSpec((None, tl, hh), lambda i, j: (i, j, 0))],
        out_specs=pl.BlockSpec((None, tl, hh), lambda i, j: (i, j, 0)),
        scratch_shapes=[pltpu.VMEM((1, hh), F32)],
        compiler_params=_cparams(("arbitrary", "arbitrary")),
        name="cumsum_time",
    )(x)


def _flash_kernel(qt_ref, kt_ref, q_ref, k_ref, v_ref, ck_ref, o_ref, m_s, l_s, acc_s, *, tq, rc):
    p = pl.program_id(1)
    qi = qt_ref[p]
    kj = kt_ref[p]

    @pl.when(kj == 0)
    def _():
        m_s[...] = jnp.full_like(m_s, -jnp.inf)
        l_s[...] = jnp.zeros_like(l_s)
        acc_s[...] = jnp.zeros_like(acc_s)

    def update(rows, diagonal):
        s = (lax.dot_general(q_ref[rows, :], k_ref[...], (((1,), (1,)), ((), ())), preferred_element_type=F32)
             - ck_ref[...] * LOG2E)
        if diagonal:
            row = lax.broadcasted_iota(jnp.int32, s.shape, 0) + rows.start
            col = lax.broadcasted_iota(jnp.int32, s.shape, 1)
            s = jnp.where(col <= row, s, -jnp.inf)
        m_prev = m_s[rows, :]
        m_new = jnp.maximum(m_prev, jnp.max(s, axis=1, keepdims=True))
        a = jnp.exp2(m_prev - m_new)
        pexp = jnp.exp2(s - m_new)
        l_s[rows, :] = a * l_s[rows, :] + jnp.sum(pexp, axis=1, keepdims=True)
        acc_s[rows, :] = a * acc_s[rows, :] + jnp.dot(pexp.astype(BF16), v_ref[...], preferred_element_type=F32)
        m_s[rows, :] = m_new

    chunks = [slice(c, c + rc) for c in range(0, tq, rc)]

    @pl.when(kj < qi)
    def _():
        for rows in chunks:
            update(rows, False)

    @pl.when(kj == qi)
    def _():
        for rows in chunks:
            update(rows, True)
        o_ref[...] = (acc_s[...] / l_s[...]).astype(o_ref.dtype)


def _flash_prefill(q, k, v, cum_t, *, t_len):
    heads, rtot, hd = q.shape
    d = heads * hd
    tq = _pick(t_len, (1024, 512, 256, 128))
    nq = t_len // tq
    pairs = [(i, j) for i in range(nq) for j in range(i + 1)]
    qt = jnp.asarray([a for a, _ in pairs], jnp.int32)
    kt = jnp.asarray([b for _, b in pairs], jnp.int32)
    grid_spec = pltpu.PrefetchScalarGridSpec(
        num_scalar_prefetch=2,
        grid=(heads, len(pairs)),
        in_specs=[pl.BlockSpec((None, tq, hd), lambda h, p, qt, kt: (h, qt[p], 0)),
                  pl.BlockSpec((None, tq, hd), lambda h, p, qt, kt: (h, kt[p], 0)),
                  pl.BlockSpec((None, tq, hd), lambda h, p, qt, kt: (h, kt[p], 0)),
                  pl.BlockSpec((None, 1, tq), lambda h, p, qt, kt: (h, 0, kt[p]))],
        out_specs=pl.BlockSpec((tq, hd), lambda h, p, qt, kt: (qt[p], h)),
        scratch_shapes=[pltpu.VMEM((tq, 1), F32), pltpu.VMEM((tq, 1), F32), pltpu.VMEM((tq, hd), F32)],
    )
    return pl.pallas_call(
        functools.partial(_flash_kernel, tq=tq, rc=_pick(tq, (256, 128))),
        out_shape=jax.ShapeDtypeStruct((rtot, d), BF16),
        grid_spec=grid_spec,
        compiler_params=_cparams(("arbitrary", "arbitrary")),
        name="fox_prefill",
    )(qt, kt, q, k, v, cum_t)


def _decode_kernel(q_ref, kn_ref, vn_ref, kc_ref, vc_ref, ck_ref, o_ref, kbuf, vbuf, sem, *,
                   past, s_len, hpb, hd, heads, layer):
    b = pl.program_id(0)
    g = pl.program_id(1)
    ng = pl.num_programs(1)
    step = b * ng + g
    slot = lax.rem(step, 2)

    def head_copies(bb, gg, s):
        out = []
        for hh in range(hpb):
            head = gg * hpb + hh
            out.append(pltpu.make_async_copy(kc_ref.at[layer, bb, :, head, :], kbuf.at[s, hh], sem.at[s]))
            out.append(pltpu.make_async_copy(vc_ref.at[layer, bb, :, head, :], vbuf.at[s, hh], sem.at[s]))
        return out

    @pl.when(step == 0)
    def _():
        for c in head_copies(0, 0, 0):
            c.start()

    for c in head_copies(b, g, slot):
        c.wait()

    @pl.when(step + 1 < pl.num_programs(0) * ng)
    def _():
        nxt = step + 1
        nb_ = nxt // ng
        for c in head_copies(nb_, nxt - nb_ * ng, 1 - slot):
            c.start()

    nt = (((1,), (1,)), ((), ()))
    row = lax.broadcasted_iota(jnp.int32, (s_len, s_len), 0)
    col = lax.broadcasted_iota(jnp.int32, (s_len, s_len), 1)
    for hh in range(hpb):
        cols = slice(hh * hd, (hh + 1) * hd)
        q = q_ref[hh]
        ck = ck_ref[hh] * LOG2E
        kc = kbuf[slot, hh].astype(BF16)
        vc = vbuf[slot, hh].astype(BF16)
        s_c = lax.dot_general(q, kc, nt, preferred_element_type=F32) - ck[:, :past]
        s_n = lax.dot_general(q, kn_ref[hh], nt, preferred_element_type=F32) - ck[:, past:]
        s_n = jnp.where(col <= row, s_n, -jnp.inf)
        m = jnp.maximum(jnp.max(s_c, axis=1, keepdims=True), jnp.max(s_n, axis=1, keepdims=True))
        p_c = jnp.exp2(s_c - m)
        p_n = jnp.exp2(s_n - m)
        l = jnp.sum(p_c, axis=1, keepdims=True) + jnp.sum(p_n, axis=1, keepdims=True)
        acc = (jnp.dot(p_c.astype(BF16), vc, preferred_element_type=F32)
               + jnp.dot(p_n.astype(BF16), vn_ref[hh], preferred_element_type=F32))
        o_ref[:, cols] = (acc / l).astype(o_ref.dtype)


def _fox_decode(q, k, v, k_cache, v_cache, cum_t, *, row_off, nseq, s_len, layer, prev):
    heads, rtot, hd = q.shape
    d = heads * hd
    nlyr, nbatch, past = k_cache.shape[:3]
    assert row_off % s_len == 0 and past % LANES == 0
    boff = row_off // s_len
    hpb = _pick(heads, (4, 2, 1))
    new_spec = pl.BlockSpec((hpb, s_len, hd), lambda b, h: (h, boff + b, 0))
    out_spec = pl.BlockSpec((s_len, hpb * hd), lambda b, h: (boff + b, h))
    any_spec = pl.BlockSpec(memory_space=pl.ANY)
    in_specs = [new_spec, new_spec, new_spec, any_spec, any_spec,
                pl.BlockSpec((None, hpb, 1, past + s_len), lambda b, h: (b, h, 0, 0))]
    args = [q, k, v, k_cache, v_cache, cum_t]
    n_in, aliases = _with_prev(args, in_specs, prev)
    kern = functools.partial(_decode_kernel, past=past, s_len=s_len, hpb=hpb, hd=hd, heads=heads, layer=layer)
    return pl.pallas_call(
        _drop_aliased(kern, n_in, len(aliases)),
        out_shape=[jax.ShapeDtypeStruct((rtot, d), BF16)],
        grid=(nseq, heads // hpb),
        in_specs=in_specs,
        out_specs=[out_spec],
        scratch_shapes=[pltpu.VMEM((2, hpb, past, hd), F32), pltpu.VMEM((2, hpb, past, hd), F32),
                        pltpu.SemaphoreType.DMA((2,))],
        input_output_aliases=aliases,
        compiler_params=_cparams(("arbitrary", "arbitrary")),
        name="fox_decode",
    )(*args)[0]


def _conv_kernel(gb_ref, gc_ref, hin_ref, cw_ref, buf_ref, o_ref, zl_ref, carry, *, tl):
    t = pl.program_id(2)

    @pl.when(t == 0)
    def _():
        carry[...] = buf_ref[...]

    z = gc_ref[...] * hin_ref[...]
    c0 = carry[0:1, :]
    c1 = carry[1:2, :]
    row = lax.broadcasted_iota(jnp.int32, z.shape, 0)
    zm1 = jnp.where(row == 0, c1, pltpu.roll(z, 1, axis=0))
    zm2 = jnp.where(row == 0, c0, jnp.where(row == 1, c1, pltpu.roll(z, 2, axis=0)))
    cw = cw_ref[...]
    acc = cw[0:1, :] * zm2 + cw[1:2, :] * zm1 + cw[2:3, :] * z
    o_ref[...] = (gb_ref[...] * acc).astype(o_ref.dtype)
    last = z[tl - 2:tl, :]
    carry[...] = last
    zl_ref[...] = last


def _conv_call(proj, conv_w, layer, buf, *, row_off, nseq, t_len, d, prev=None, name="conv"):
    rtot = proj.shape[0]
    assert conv_w.shape[1] == 3 and t_len >= 2
    tn = _pick(d, (1024, 512, 256, 128))
    tl = _pick(t_len, (256, 128, 64, 32, 16, 8))
    nt = t_len // tl
    nd = d // tn
    assert row_off % tl == 0
    boff = row_off // tl

    def pspec(part):
        return pl.BlockSpec((tl, tn), lambda b, j, t: (boff + b * nt + t, part * nd + j))

    in_specs = [pspec(0), pspec(1), pspec(2),
                pl.BlockSpec((None, 3, tn), lambda b, j, t: (layer, 0, j)),
                pl.BlockSpec((None, 2, tn), lambda b, j, t: (b, 0, j))]
    args = [proj, proj, proj, conv_w, buf]
    n_in, aliases = _with_prev(args, in_specs, prev)
    return pl.pallas_call(
        _drop_aliased(functools.partial(_conv_kernel, tl=tl), n_in, len(aliases)),
        out_shape=[jax.ShapeDtypeStruct((rtot, d), BF16), jax.ShapeDtypeStruct((nseq, 2, d), F32)],
        grid=(nseq, nd, nt),
        in_specs=in_specs,
        out_specs=[pl.BlockSpec((tl, tn), lambda b, j, t: (boff + b * nt + t, j)),
                   pl.BlockSpec((None, 2, tn), lambda b, j, t: (b, 0, j))],
        scratch_shapes=[pltpu.VMEM((2, tn), F32)],
        input_output_aliases=aliases,
        compiler_params=_cparams(("arbitrary", "arbitrary", "arbitrary")),
        name=name,
    )(*args)


WEIGHT_LOOKAHEAD = 2
WEIGHT_RING = WEIGHT_LOOKAHEAD + 1


def _expert_kernel(be_ref, nu_ref, ss_ref, nr_ref, tok_ref, h_ref, wg_ref, wu_ref, wd_ref, o_ref,
                   xbuf, sem, gu_buf, gu_sem, d_buf, d_sem, hg, hu, hb, *, layer, nkt, nnt, tk, tn):
    b = pl.program_id(0)
    j = pl.program_id(1)
    nu = nu_ref[0]
    slot = lax.rem(b, 2)
    nsteps = nkt + nnt

    def row_copy(s, r, tok):
        return pltpu.make_async_copy(h_ref.at[pl.ds(tok, 1)], xbuf.at[s, pl.ds(r, 1)], sem.at[s])

    def start_block(bb, s):
        base = ss_ref[bb]

        def body(r, c):
            row_copy(s, r, tok_ref[base + r]).start()
            return c
        lax.fori_loop(0, nr_ref[bb], body, 0)

    def wait_block(bb, s):
        n = nr_ref[bb]

        def body8(r, c):
            pltpu.make_async_copy(h_ref.at[pl.ds(0, SUBLANES)], xbuf.at[s, pl.ds(0, SUBLANES)], sem.at[s]).wait()
            return c

        def body1(r, c):
            row_copy(s, 0, 0).wait()
            return c
        lax.fori_loop(0, lax.shift_right_logical(n, 3), body8, 0)
        lax.fori_loop(0, lax.bitwise_and(n, SUBLANES - 1), body1, 0)

    def gu_copies(e, jt, ring_slot):
        k0 = pl.multiple_of(jt * tk, tk)
        return (pltpu.make_async_copy(wg_ref.at[layer, e, pl.ds(k0, tk), :], gu_buf.at[ring_slot, 0],
                                      gu_sem.at[ring_slot]),
                pltpu.make_async_copy(wu_ref.at[layer, e, pl.ds(k0, tk), :], gu_buf.at[ring_slot, 1],
                                      gu_sem.at[ring_slot]))

    def d_copy(e, nt, ring_slot):
        n0 = pl.multiple_of(nt * tn, tn)
        return pltpu.make_async_copy(wd_ref.at[layer, e, :, pl.ds(n0, tn)], d_buf.at[ring_slot],
                                     d_sem.at[ring_slot])

    def start_tiles(step):
        b2 = step // nsteps
        j2 = step - b2 * nsteps

        @pl.when(b2 < nu)
        def _():
            e2 = be_ref[b2]

            @pl.when(j2 < nkt)
            def _():
                for c in gu_copies(e2, j2, lax.rem(b2 * nkt + j2, WEIGHT_RING)):
                    c.start(priority=1)

            @pl.when(j2 >= nkt)
            def _():
                d_copy(e2, j2 - nkt, lax.rem(b2 * nnt + j2 - nkt, WEIGHT_RING)).start(priority=1)

    step = b * nsteps + j

    @pl.when(step == 0)
    def _():
        xbuf[...] = jnp.zeros_like(xbuf)
        start_block(0, 0)
        for ahead in range(WEIGHT_LOOKAHEAD):
            start_tiles(ahead)

    @pl.when(b < nu)
    def _():
        start_tiles(step + WEIGHT_LOOKAHEAD)

    @pl.when((b < nu) & (j == 0))
    def _():
        wait_block(b, slot)

        @pl.when(b + 1 < nu)
        def _():
            start_block(b + 1, 1 - slot)

    @pl.when(b < nu)
    def _():
        for jj in range(nkt):
            @pl.when(j == jj)
            def _(jj=jj):
                ring_slot = lax.rem(b * nkt + jj, WEIGHT_RING)
                for c in gu_copies(0, 0, ring_slot):
                    c.wait()
                xb = xbuf[slot, :, jj * tk:(jj + 1) * tk].astype(BF16)
                pg = jnp.dot(xb, gu_buf[ring_slot, 0].astype(BF16), preferred_element_type=F32)
                pu = jnp.dot(xb, gu_buf[ring_slot, 1].astype(BF16), preferred_element_type=F32)
                if jj > 0:
                    pg = hg[...] + pg
                    pu = hu[...] + pu
                if jj < nkt - 1:
                    hg[...] = pg
                    hu[...] = pu
                else:
                    hb[...] = (pg * jax.nn.sigmoid(pg) * pu).astype(BF16)

        @pl.when(j >= nkt)
        def _():
            ring_slot = lax.rem(b * nnt + j - nkt, WEIGHT_RING)
            d_copy(0, 0, ring_slot).wait()
            o_ref[...] = jnp.dot(hb[...], d_buf[ring_slot].astype(BF16), preferred_element_type=F32)


def _expert_call(h, plan, w_gate, w_up, w_down, layer, *, tm):
    block_expert, n_used, src_start, nrows, sorted_tok = plan
    d = h.shape[1]
    de = w_gate.shape[-1]
    nb = block_expert.shape[0]
    rows = nb * tm
    tk = _pick(d, (1024, 512, 256, 128))
    tn = tk
    nkt = d // tk
    nnt = d // tn
    last_n = nnt - 1

    def blk(b, nu):
        return jnp.minimum(b, nu[0] - 1)

    def nidx(b, j, nu):
        return jnp.where(b < nu[0], jnp.maximum(j - nkt, 0), last_n)

    any_spec = pl.BlockSpec(memory_space=pl.ANY)
    grid_spec = pltpu.PrefetchScalarGridSpec(
        num_scalar_prefetch=5,
        grid=(nb, nkt + nnt),
        in_specs=[any_spec, any_spec, any_spec, any_spec],
        out_specs=pl.BlockSpec((tm, tn), lambda b, j, be, nu, ss, nr, tok: (blk(b, nu), nidx(b, j, nu))),
        scratch_shapes=[pltpu.VMEM((2, tm, d), F32), pltpu.SemaphoreType.DMA((2,)),
                        pltpu.VMEM((WEIGHT_RING, 2, tk, de), F32), pltpu.SemaphoreType.DMA((WEIGHT_RING,)),
                        pltpu.VMEM((WEIGHT_RING, de, tn), F32), pltpu.SemaphoreType.DMA((WEIGHT_RING,)),
                        pltpu.VMEM((tm, de), F32), pltpu.VMEM((tm, de), F32), pltpu.VMEM((tm, de), BF16)],
    )
    return pl.pallas_call(
        functools.partial(_expert_kernel, layer=layer, nkt=nkt, nnt=nnt, tk=tk, tn=tn),
        out_shape=jax.ShapeDtypeStruct((rows, d), F32),
        grid_spec=grid_spec,
        compiler_params=_cparams(("arbitrary", "arbitrary")),
        name="moe_experts",
    )(block_expert, n_used, src_start, nrows, sorted_tok, h, w_gate, w_up, w_down)


def _dispatch(ids, n_experts, tm):
    r, k = ids.shape
    a = r * k
    flat_e = ids.reshape(a)
    order = jnp.argsort(flat_e, stable=True).astype(jnp.int32)
    inv = jnp.argsort(order).astype(jnp.int32)
    onehot = flat_e[:, None] == jnp.arange(n_experts, dtype=jnp.int32)[None, :]
    counts = jnp.sum(onehot, axis=0, dtype=jnp.int32)
    padded = (counts + tm - 1) // tm * tm
    pad_end = jnp.cumsum(padded)
    pad_start = pad_end - padded
    start = jnp.cumsum(counts) - counts
    shift = jnp.sum(jnp.where(onehot, (pad_start - start)[None, :], 0), axis=1, dtype=jnp.int32)
    slot_of_assign = (inv + shift).reshape(r, k)
    nb = -(-a // tm) + n_experts
    blk_row = jnp.arange(nb, dtype=jnp.int32) * tm
    block_expert = jnp.minimum(jnp.sum(pad_end[None, :] <= blk_row[:, None], axis=1, dtype=jnp.int32),
                               n_experts - 1)
    q = blk_row - pad_start[block_expert]
    nrows = jnp.clip(counts[block_expert] - q, 0, tm).astype(jnp.int32)
    src_start = jnp.clip(start[block_expert] + q, 0, a - 1).astype(jnp.int32)
    n_used = (pad_end[-1] // tm).astype(jnp.int32).reshape(1)
    sorted_tok = order // k
    return (block_expert, n_used, src_start, nrows, sorted_tok), slot_of_assign


def _moe(h, route, w_gate, w_up, w_down, layer):
    n_experts = w_gate.shape[1]
    ids = route[:, 0:2].astype(jnp.int32)
    a = ids.size
    tm = _pick(a, (512, 256, 128, 64, 32, 16, 8))
    plan, slot_of_assign = _dispatch(ids, n_experts, tm)
    ys = _expert_call(h, plan, w_gate, w_up, w_down, layer, tm=tm)
    return ys, slot_of_assign.T.reshape(a)


def kernel(x_prompt, x_sample, c_prompt, c_sample, state_ssm_re, state_ssm_im, cache_fox_k, cache_fox_v, cache_fox_logf, state_conv, ssm_lam_re, ssm_lam_im, ssm_log_dt, ssm_b_re, ssm_b_im, ssm_c_re, ssm_c_im, ssm_d, ssm_w_glu, ssm_b_glu, fox_w_qkvf, fox_b_f, fox_w_o, conv_w_in, conv_w, conv_w_out, ada_w, ada_b, ln_g, ln_b, moe_w_group, moe_b_group, moe_w_expert, moe_b_expert, moe_w_gate, moe_w_up, moe_w_down):
    bp, seq, d = x_prompt.shape
    nbs, dseq, _ = x_sample.shape
    depth = ada_w.shape[0]
    heads, hd = cache_fox_k.shape[3], cache_fox_k.shape[4]
    past = cache_fox_k.shape[2]
    n_groups = moe_w_group.shape[-1]
    n_experts = moe_w_expert.shape[-1]
    assert bp == 1 and n_groups == 8 and n_experts == 64 and heads * hd == d
    alpha = float((2 * depth) ** 0.25)
    rp, rs = bp * seq, nbs * dseq
    geom = dict(rp=rp, rs=rs, seq=seq, dseq=dseq, nbs=nbs)

    x = (x_prompt.reshape(rp, d), x_sample.reshape(rs, d))

    nmod = -(-(nbs + bp) // SUBLANES) * SUBLANES
    c_all = jnp.concatenate([c_sample, c_prompt, jnp.zeros((nmod - nbs - bp, d), F32)], axis=0)
    mods = jnp.stack([_matmul(c_all, ada_w, i, n_out=6 * d, bias=ada_b, lhs_silu=True, name="ada")[0]
                      for i in range(depth)]).reshape(depth, nmod, 1, 6 * d)
    ln_g4 = ln_g.reshape(depth, 2, 1, d)
    ln_b4 = ln_b.reshape(depth, 2, 1, d)
    SH1, SC1, G1, SH2, SC2, G2 = range(6)

    def mixer_h_dtype(i):
        return F32 if i % N_MIXERS == 0 else BF16

    def router_weights(i):
        w = jnp.concatenate([moe_w_group[i], moe_w_expert[i],
                             jnp.zeros((d, LANES - n_groups - n_experts), F32)], axis=1)
        b = jnp.concatenate([moe_b_group[i], moe_b_expert[i],
                             jnp.zeros((LANES - n_groups - n_experts,), F32)]).reshape(1, LANES)
        w_hi = w.astype(BF16)
        w_lo = (w - w_hi.astype(F32)).astype(BF16)
        return w_hi, w_lo, b

    (h,) = _ln_both(x, mods, geom, h_mods=(0, SC1, SH1), h_dtype=mixer_h_dtype(0))

    re_p, im_p, re_s, im_s = [], [], [], []
    k_p, v_p, lf_p, k_s, v_s, lf_s = [], [], [], [], [], []
    conv_p, conv_s = [], []
    for i in range(depth):
        j = i // N_MIXERS
        kind = i % N_MIXERS
        if kind == 0:
            wts = _s5_weights(ssm_lam_re[j].astype(F32), ssm_lam_im[j].astype(F32), ssm_log_dt[j],
                              ssm_b_re[j].astype(F32), ssm_b_im[j].astype(F32), ssm_c_re[j], ssm_c_im[j])
            nstate = wts[2].shape[0]
            zero_state = jnp.zeros((bp, nstate, LANES), F32)
            d_skip = ssm_d[j].reshape(1, d)
            outs = _s5_call(h, wts, d_skip, zero_state, zero_state, row_off=0, nseq=bp, t_len=seq,
                            name="s5_prompt")
            outs2 = _s5_call(h, wts, d_skip, state_ssm_re[j].reshape(nbs, nstate, LANES),
                             state_ssm_im[j].reshape(nbs, nstate, LANES), row_off=rp, nseq=nbs,
                             t_len=dseq, prev=[outs[0]], name="s5_sample")
            gshape = ssm_lam_re.shape[1:]
            re_p.append(outs[1].reshape((bp,) + gshape))
            im_p.append(outs[2].reshape((bp,) + gshape))
            re_s.append(outs2[1].reshape((nbs,) + gshape))
            im_s.append(outs2[2].reshape((nbs,) + gshape))
            (out,) = _matmul(outs2[0], ssm_w_glu, j, n_out=d, bias=ssm_b_glu, glu=True, name="s5_glu")
        elif kind == 1:
            (qh,) = _matmul(h, fox_w_qkvf, j, n_out=d, col_off=0, row_out=False, heads_hd=hd,
                            heads_scale=float(hd) ** -0.5 * LOG2E, name="fox_q")
            k, kh = _matmul(h, fox_w_qkvf, j, n_out=d, col_off=d, heads_hd=hd, name="fox_k")
            v, vh = _matmul(h, fox_w_qkvf, j, n_out=d, col_off=2 * d, heads_hd=hd, name="fox_v")
            w_f = fox_w_qkvf[:, :, 3 * d:]
            (lf,) = _matmul(h, w_f, j, n_out=heads, bias=fox_b_f, epilogue="logsigmoid", name="fox_f")
            lf_prompt = lf[:rp].reshape(bp, seq, heads)
            lf_sample = lf[rp:].reshape(nbs, dseq, heads)
            cum_p = _cumsum_time(lf_prompt)
            cum_s = _cumsum_time(jnp.concatenate([cache_fox_logf[j].astype(F32), lf_sample], axis=1))
            o_p = _flash_prefill(qh, kh, vh, cum_p[0].T.reshape(heads, 1, seq), t_len=seq)
            o = _fox_decode(qh, kh, vh, cache_fox_k, cache_fox_v,
                            cum_s.transpose(0, 2, 1).reshape(nbs, heads, 1, past + dseq),
                            row_off=rp, nseq=nbs, s_len=dseq, layer=j, prev=[o_p])
            (out,) = _matmul(o, fox_w_o, j, n_out=d, name="fox_o")
            k_p.append(k[:rp].reshape(bp, seq, heads, hd))
            v_p.append(v[:rp].reshape(bp, seq, heads, hd))
            lf_p.append(lf_prompt)
            k_s.append(k[rp:].reshape(nbs, dseq, heads, hd))
            v_s.append(v[rp:].reshape(nbs, dseq, heads, hd))
            lf_s.append(lf_sample)
        else:
            (proj,) = _matmul(h, conv_w_in, j, n_out=3 * d, name="conv_in")
            zero_buf = jnp.zeros((bp, conv_w.shape[1] - 1, d), F32)
            outs = _conv_call(proj, conv_w, j, zero_buf, row_off=0, nseq=bp, t_len=seq, d=d, name="conv_prompt")
            outs2 = _conv_call(proj, conv_w, j, state_conv[j].astype(F32), row_off=rp, nseq=nbs, t_len=dseq,
                               d=d, prev=[outs[0]], name="conv_sample")
            conv_p.append(outs[1])
            conv_s.append(outs2[1])
            (out,) = _matmul(outs2[0], conv_w_out, j, n_out=d, name="conv_out")

        x, h2, route = _ln_both(x, mods, geom, alpha=alpha, o=out, res_mod=(i, G1), ln_g=ln_g4, ln_b=ln_b4,
                                ln_idx=(i, 0), h_mods=(i, SC2, SH2), h_dtype=F32,
                                router_w=router_weights(i))
        ys, slots = _moe(h2, route, moe_w_gate, moe_w_up, moe_w_down, i)
        ffn = dict(alpha=alpha, o=ys, slots=slots, route_in=route, res_mod=(i, G2),
                   ln_g=ln_g4, ln_b=ln_b4, ln_idx=(i, 1))
        if i + 1 < depth:
            x, h = _ln_both(x, mods, geom, h_mods=(i + 1, SC1, SH1), h_dtype=mixer_h_dtype(i + 1), **ffn)
        else:
            (x,) = _ln_both(x, mods, geom, **ffn)

    y_prompt = x[:rp].reshape(bp, seq, d)
    y_sample = x[rp:].reshape(nbs, dseq, d)
    return (y_prompt, y_sample, jnp.stack(re_p), jnp.stack(im_p), jnp.stack(k_p), jnp.stack(v_p),
            jnp.stack(lf_p), jnp.stack(conv_p), jnp.stack(re_s), jnp.stack(im_s), jnp.stack(k_s),
            jnp.stack(v_s), jnp.stack(lf_s), jnp.stack(conv_s))
```

```python
import functools

import jax
import jax.numpy as jnp
from jax import lax
from jax.experimental import pallas as pl
from jax.experimental.pallas import tpu as pltpu

LN_EPS = 1e-5
N_MIXERS = 3
LOG2E = 1.4426950408889634

V7X_VMEM_BYTES = 64 * 1024 * 1024
VMEM_LIMIT_BYTES = V7X_VMEM_BYTES - 8 * 1024 * 1024
LANES = 128
SUBLANES = 8

F32 = jnp.float32
BF16 = jnp.bfloat16


def _pick(n, candidates):
    for c in candidates:
        if c <= n and n % c == 0:
            return c
    return n


def _cparams(sem):
    return pltpu.CompilerParams(dimension_semantics=sem, vmem_limit_bytes=VMEM_LIMIT_BYTES)


def _drop_aliased(kernel, n_in, n_alias):
    def wrapped(*refs):
        kernel(*(refs[:n_in] + refs[n_in + n_alias:]))
    return wrapped


def _with_prev(args, in_specs, prev):
    aliases = {}
    n_in = len(args)
    if prev is not None:
        for k, p in enumerate(prev):
            args.append(p)
            in_specs.append(pl.BlockSpec(memory_space=pl.ANY))
            aliases[n_in + k] = k
    return n_in, aliases


def _mm_kernel(*refs, nk, glu, has_bias, epilogue, lhs_silu, row_out, heads_hd, heads_scale):
    it = iter(refs)
    x_ref = next(it)
    w_ref = next(it)
    w2_ref = next(it) if glu else None
    b_ref = next(it) if has_bias else None
    b2_ref = next(it) if (has_bias and glu) else None
    o_ref = next(it) if row_out else None
    oh_ref = next(it) if heads_hd else None
    acc_ref = next(it)
    acc2_ref = next(it) if glu else None
    k = pl.program_id(2)

    def partial_sums(first):
        x = x_ref[...]
        if lhs_silu:
            xf = x.astype(F32)
            x = xf * jax.nn.sigmoid(xf)
        xb = x.astype(BF16)
        z = jnp.dot(xb, w_ref[...].astype(BF16), preferred_element_type=F32)
        z2 = jnp.dot(xb, w2_ref[...].astype(BF16), preferred_element_type=F32) if glu else None
        if not first:
            z = acc_ref[...] + z
            if glu:
                z2 = acc2_ref[...] + z2
        return z, z2

    def keep(z, z2):
        acc_ref[...] = z
        if glu:
            acc2_ref[...] = z2

    def finish(z, z2):
        if has_bias:
            z = z + b_ref[...]
        if glu:
            if has_bias:
                z2 = z2 + b2_ref[...]
            z = z * jax.nn.sigmoid(z2)
        if epilogue == "logsigmoid":
            z = jax.nn.log_sigmoid(z)
        if row_out:
            o_ref[...] = z.astype(o_ref.dtype)
        if heads_hd:
            for hh in range(z.shape[1] // heads_hd):
                zh = z[:, hh * heads_hd:(hh + 1) * heads_hd]
                oh_ref[hh] = (zh * heads_scale if heads_scale != 1.0 else zh).astype(oh_ref.dtype)

    if nk == 1:
        finish(*partial_sums(True))
        return

    @pl.when(k == 0)
    def _():
        keep(*partial_sums(True))

    if nk > 2:
        @pl.when((k > 0) & (k < nk - 1))
        def _():
            keep(*partial_sums(False))

    @pl.when(k == nk - 1)
    def _():
        finish(*partial_sums(False))


def _matmul(x, w, layer, *, n_out, col_off=0, bias=None, glu=False, epilogue=None,
            lhs_silu=False, out_dtype=F32, row_out=True, heads_hd=None, heads_scale=1.0, name="mm"):
    m, kdim = x.shape
    nw = w.shape[-1]
    if m <= 64:
        tm = m
        tn = _pick(n_out, (2048, 1024, 512, 256, 128))
        tk = _pick(kdim, (1024, 512, 256, 128))
    else:
        tm = _pick(m, (1408, 1024, 768, 512, 256, 128, 64, 32, 16, 8))
        tn = _pick(n_out, (512, 256, 128) if glu else (1024, 512, 256, 128))
        tk = _pick(kdim, (1024, 512, 256, 128))
    assert col_off % tn == 0 and n_out % tn == 0
    nk = kdim // tk
    coff = col_off // tn
    goff = (col_off + n_out) // tn
    in_specs = [pl.BlockSpec((tm, tk), lambda i, j, k: (i, k)),
                pl.BlockSpec((None, tk, tn), lambda i, j, k: (layer, k, j + coff))]
    args = [x, w]
    if glu:
        in_specs.append(pl.BlockSpec((None, tk, tn), lambda i, j, k: (layer, k, j + goff)))
        args.append(w)
    if bias is not None:
        b3 = bias.reshape(bias.shape[0], 1, nw)
        in_specs.append(pl.BlockSpec((None, 1, tn), lambda i, j, k: (layer, 0, j + coff)))
        args.append(b3)
        if glu:
            in_specs.append(pl.BlockSpec((None, 1, tn), lambda i, j, k: (layer, 0, j + goff)))
            args.append(b3)
    out_shape, out_specs = [], []
    if row_out:
        out_shape.append(jax.ShapeDtypeStruct((m, n_out), out_dtype))
        out_specs.append(pl.BlockSpec((tm, tn), lambda i, j, k: (i, j)))
    if heads_hd:
        assert tn % heads_hd == 0
        out_shape.append(jax.ShapeDtypeStruct((n_out // heads_hd, m, heads_hd), BF16))
        out_specs.append(pl.BlockSpec((tn // heads_hd, tm, heads_hd), lambda i, j, k: (j, i, 0)))
    scratch = [pltpu.VMEM((tm, tn), F32)]
    if glu:
        scratch.append(pltpu.VMEM((tm, tn), F32))
    kern = functools.partial(_mm_kernel, nk=nk, glu=glu, has_bias=bias is not None, epilogue=epilogue,
                             lhs_silu=lhs_silu, row_out=row_out, heads_hd=heads_hd, heads_scale=heads_scale)
    return pl.pallas_call(
        kern,
        out_shape=out_shape,
        grid=(m // tm, n_out // tn, nk),
        in_specs=in_specs,
        out_specs=out_specs,
        scratch_shapes=scratch,
        compiler_params=_cparams(("parallel", "parallel", "arbitrary")),
        name=name,
    )(*args)


def _cast_kernel(w_ref, o_ref):
    o_ref[...] = w_ref[...].astype(o_ref.dtype)


def _weight_bf16(w, layer, n_cols):
    kdim = w.shape[1]
    tk = _pick(kdim, (512, 256, 128))
    tn = _pick(n_cols, (2048, 1024, 512, 256, 128))
    return pl.pallas_call(
        _cast_kernel,
        out_shape=jax.ShapeDtypeStruct((1, kdim, n_cols), BF16),
        grid=(kdim // tk, n_cols // tn),
        in_specs=[pl.BlockSpec((None, tk, tn), lambda i, j: (layer, i, j))],
        out_specs=pl.BlockSpec((None, tk, tn), lambda i, j: (0, i, j)),
        compiler_params=_cparams(("parallel", "parallel")),
        name="weight_bf16",
    )(w)


def _expand_mod(ref, nb, rpm):
    v = ref[...]
    d = v.shape[-1]
    if nb == 1:
        return v.reshape(1, d)
    return jnp.broadcast_to(v, (nb, rpm, d)).reshape(nb * rpm, d)


def _route(logits):
    tm = logits.shape[0]
    lane = lax.broadcasted_iota(jnp.int32, (tm, LANES), 1)
    neg = jnp.float32(-jnp.inf)
    big = jnp.int32(LANES)
    glog = jnp.where(lane < 8, logits, neg)
    gmax = jnp.max(glog, axis=1, keepdims=True)
    gidx = jnp.min(jnp.where(glog == gmax, lane, big), axis=1, keepdims=True)
    gsum = jnp.sum(jnp.exp(glog - gmax), axis=1, keepdims=True)
    g_w = 1.0 / gsum
    emask = (lane >= 8) & (lane < 72) & (((lane - 8) >> 3) == gidx)
    elog = jnp.where(emask, logits, neg)
    emax = jnp.max(elog, axis=1, keepdims=True)
    i1 = jnp.min(jnp.where(elog == emax, lane, big), axis=1, keepdims=True)
    elog2 = jnp.where(lane == i1, neg, elog)
    emax2 = jnp.max(elog2, axis=1, keepdims=True)
    i2 = jnp.min(jnp.where(elog2 == emax2, lane, big), axis=1, keepdims=True)
    esum = jnp.sum(jnp.exp(elog - emax), axis=1, keepdims=True)
    p1 = 1.0 / esum
    p2 = jnp.exp(emax2 - emax) / esum
    psum = p1 + p2
    gate0 = g_w * p1 / psum
    gate1 = g_w * p2 / psum
    return jnp.where(lane == 0, (i1 - 8).astype(F32),
                     jnp.where(lane == 1, (i2 - 8).astype(F32),
                               jnp.where(lane == 2, gate0, jnp.where(lane == 3, gate1, 0.0))))


def _ln_kernel(*refs, alpha, nb, rpm, first, combine, emit_h, router, tm, boff, rtot):
    it = iter(refs)
    slots_ref = next(it) if combine else None
    x_ref = next(it)
    if not first:
        o_ref = next(it)
        if combine:
            rt_in_ref = next(it)
        g_ref = next(it)
        lg_ref = next(it)
        lb_ref = next(it)
    if emit_h:
        sc_ref = next(it)
        sh_ref = next(it)
    if router:
        whi_ref = next(it)
        wlo_ref = next(it)
        br_ref = next(it)
    xo_ref = None if first else next(it)
    h_ref = next(it) if emit_h else None
    rt_ref = next(it) if router else None
    if combine:
        ybuf = next(it)
        sem = next(it)

    x = x_ref[...].astype(F32)
    if not first:
        if combine:
            i = pl.program_id(0)
            slot = lax.rem(i, 2)

            def start(ii, s):
                base = (ii + boff) * tm

                def body(r, c):
                    for choice in range(2):
                        src_row = slots_ref[choice * rtot + base + r]
                        pltpu.make_async_copy(o_ref.at[pl.ds(src_row, 1)], ybuf.at[s, choice, pl.ds(r, 1)],
                                              sem.at[s]).start()
                    return c
                lax.fori_loop(0, tm, body, 0, unroll=8)

            @pl.when(i == 0)
            def _():
                start(0, 0)

            for choice in range(2):
                pltpu.make_async_copy(o_ref.at[pl.ds(0, tm)], ybuf.at[slot, choice], sem.at[slot]).wait()

            @pl.when(i + 1 < pl.num_programs(0))
            def _():
                start(i + 1, 1 - slot)

            rt = rt_in_ref[...]
            o = rt[:, 2:3] * ybuf[slot, 0] + rt[:, 3:4] * ybuf[slot, 1]
        else:
            o = o_ref[...].astype(F32)
        g = _expand_mod(g_ref, nb, rpm)
        v = alpha * x + (1.0 + g) * o
        mu = jnp.mean(v, axis=-1, keepdims=True)
        vc = v - mu
        var = jnp.mean(vc * vc, axis=-1, keepdims=True)
        x = vc * lax.rsqrt(var + LN_EPS) * lg_ref[...].reshape(1, -1) + lb_ref[...].reshape(1, -1)
        xo_ref[...] = x
    if emit_h:
        sc = _expand_mod(sc_ref, nb, rpm)
        sh = _expand_mod(sh_ref, nb, rpm)
        h = x * (1.0 + sc) + sh
        h_ref[...] = h.astype(h_ref.dtype)
        if router:
            h_hi = h.astype(BF16)
            h_lo = (h - h_hi.astype(F32)).astype(BF16)
            whi = whi_ref[...]
            logits = (jnp.dot(h_hi, whi, preferred_element_type=F32)
                      + jnp.dot(h_lo, whi, preferred_element_type=F32)
                      + jnp.dot(h_hi, wlo_ref[...], preferred_element_type=F32)
                      + br_ref[...])
            rt_ref[...] = _route(logits)


def _ln_call(x, mods, *, rtot, row_off, rows, rpm, seq0, alpha=1.0, o=None, slots=None, route_in=None,
             res_mod=None, ln_g=None, ln_b=None, ln_idx=None, h_mods=None,
             h_dtype=None, router_w=None, prev=None, name="ln"):
    d = x.shape[1]
    first = o is None
    combine = slots is not None
    emit_h = h_dtype is not None
    router = router_w is not None
    tm = _pick(rows, (128, 64, 32, 16, 8))
    if tm > rpm:
        assert tm % rpm == 0 and rpm % SUBLANES == 0
        nb = tm // rpm
        assert seq0 % nb == 0
    else:
        assert rpm % tm == 0
        nb = 1
    assert row_off % tm == 0 and rows % tm == 0
    boff = row_off // tm
    x_boff = boff if x.shape[0] == rtot else 0

    def row_map(i, *_):
        return (i + boff, 0)

    def mod_spec(lyr, chunk):
        if nb == 1:
            return pl.BlockSpec((None, 1, 1, d), lambda i, *_: (lyr, seq0 + (i * tm) // rpm, 0, chunk))
        return pl.BlockSpec((None, nb, 1, d), lambda i, *_: (lyr, seq0 // nb + i, 0, chunk))

    row_spec = pl.BlockSpec((tm, d), row_map)
    in_specs = [pl.BlockSpec((tm, d), lambda i, *_: (i + x_boff, 0))]
    args = [x]
    if not first:
        if combine:
            in_specs += [pl.BlockSpec(memory_space=pl.ANY), pl.BlockSpec((tm, LANES), row_map)]
            args += [o, route_in]
        else:
            in_specs.append(row_spec)
            args.append(o)
        in_specs.append(mod_spec(*res_mod))
        args.append(mods)
        in_specs += [pl.BlockSpec((None, None, 1, d), lambda i, *_: (ln_idx[0], ln_idx[1], 0, 0))] * 2
        args += [ln_g, ln_b]
    if emit_h:
        in_specs += [mod_spec(h_mods[0], h_mods[1]), mod_spec(h_mods[0], h_mods[2])]
        args += [mods, mods]
    if router:
        whi, wlo, br = router_w
        in_specs += [pl.BlockSpec((d, LANES), lambda i, *_: (0, 0)),
                     pl.BlockSpec((d, LANES), lambda i, *_: (0, 0)),
                     pl.BlockSpec((1, LANES), lambda i, *_: (0, 0))]
        args += [whi, wlo, br]
    out_shape = []
    out_specs = []
    if not first:
        out_shape.append(jax.ShapeDtypeStruct((rtot, d), F32))
        out_specs.append(row_spec)
    if emit_h:
        out_shape.append(jax.ShapeDtypeStruct((rtot, d), h_dtype))
        out_specs.append(row_spec)
    if router:
        out_shape.append(jax.ShapeDtypeStruct((rtot, LANES), F32))
        out_specs.append(pl.BlockSpec((tm, LANES), row_map))
    scratch = []
    if combine:
        args = [slots] + args
        scratch = [pltpu.VMEM((2, 2, tm, d), F32), pltpu.SemaphoreType.DMA((2,))]
    n_in, aliases = _with_prev(args, in_specs, prev)
    kern = functools.partial(_ln_kernel, alpha=alpha, nb=nb, rpm=rpm, first=first, combine=combine,
                             emit_h=emit_h, router=router, tm=tm, boff=boff, rtot=rtot)
    grid_spec = pltpu.PrefetchScalarGridSpec(
        num_scalar_prefetch=1 if combine else 0,
        grid=(rows // tm,),
        in_specs=in_specs,
        out_specs=out_specs,
        scratch_shapes=scratch,
    )
    outs = pl.pallas_call(
        _drop_aliased(kern, n_in, len(aliases)),
        out_shape=out_shape,
        grid_spec=grid_spec,
        input_output_aliases=aliases,
        compiler_params=_cparams(("arbitrary" if combine else "parallel",)),
        name=name,
    )(*args)
    return list(outs)


def _ln_both(xs, mods, geom, **kw):
    xp, xsm = xs if isinstance(xs, tuple) else (xs, xs)
    rtot = geom["rp"] + geom["rs"]
    outs = _ln_call(xp, mods, rtot=rtot, row_off=0, rows=geom["rp"], rpm=geom["seq"], seq0=geom["nbs"],
                    name="ln_prompt", **kw)
    return _ln_call(xsm, mods, rtot=rtot, row_off=geom["rp"], rows=geom["rs"], rpm=geom["dseq"], seq0=0,
                    prev=outs, name="ln_sample", **kw)


S5_GC = 16
S5_P = 64
S5_GPB = LANES // S5_GC
S5_CB = S5_GPB * S5_P // LANES


def _s5_kernel(u_ref, win_ref, wout_ref, are_ref, aim_ref, d_ref, h0r_ref, h0i_ref,
               y_ref, hr_ref, hi_ref, bre, bim, sre, sim, *, tl, pitch, nkb):
    t = pl.program_id(2)
    ncb = nkb * S5_CB
    half = S5_CB * LANES

    @pl.when(t == 0)
    def _():
        sre[...] = h0r_ref[...]
        sim[...] = h0i_ref[...]

    u = u_ref[...]
    ub = u.astype(BF16)
    for kk in range(nkb):
        bu = jnp.dot(ub[:, kk * LANES:(kk + 1) * LANES], win_ref[kk], preferred_element_type=F32)
        for j in range(S5_CB):
            c = kk * S5_CB + j
            bre[pl.ds(c * pitch, tl), :] = bu[:, j * LANES:(j + 1) * LANES]
            bim[pl.ds(c * pitch, tl), :] = bu[:, half + j * LANES:half + (j + 1) * LANES]

    nog = ncb // SUBLANES
    a_r = [are_ref[pl.ds(o * SUBLANES, SUBLANES), :] for o in range(nog)]
    a_i = [aim_ref[pl.ds(o * SUBLANES, SUBLANES), :] for o in range(nog)]
    s0 = tuple(sre[pl.ds(o * SUBLANES, SUBLANES), :] for o in range(nog)) + \
        tuple(sim[pl.ds(o * SUBLANES, SUBLANES), :] for o in range(nog))

    def step(tt, carry):
        new_r, new_i = [], []
        for o in range(nog):
            s_r, s_i = carry[o], carry[nog + o]
            idx = pl.ds(o * SUBLANES * pitch + tt, SUBLANES, stride=pitch)
            n_r = a_r[o] * s_r - a_i[o] * s_i + bre[idx, :]
            n_i = a_r[o] * s_i + a_i[o] * s_r + bim[idx, :]
            bre[idx, :] = n_r
            bim[idx, :] = n_i
            new_r.append(n_r)
            new_i.append(n_i)
        return tuple(new_r) + tuple(new_i)

    fin = lax.fori_loop(0, tl, step, s0, unroll=4)
    for o in range(nog):
        sre[pl.ds(o * SUBLANES, SUBLANES), :] = fin[o]
        sim[pl.ds(o * SUBLANES, SUBLANES), :] = fin[nog + o]
    hr_ref[...] = sre[...]
    hi_ref[...] = sim[...]

    ys = []
    for kk in range(nkb):
        parts = [bre[pl.ds((kk * S5_CB + j) * pitch, tl), :] for j in range(S5_CB)]
        parts += [bim[pl.ds((kk * S5_CB + j) * pitch, tl), :] for j in range(S5_CB)]
        lhs = jnp.concatenate(parts, axis=1).astype(BF16)
        ys.append(jnp.dot(lhs, wout_ref[kk], preferred_element_type=F32))
    y = jnp.concatenate(ys, axis=1) if nkb > 1 else ys[0]
    yy = y + d_ref[...] * u
    y_ref[...] = jax.nn.gelu(yy).astype(y_ref.dtype)


def _s5_weights(lam_re, lam_im, log_dt, b_re, b_im, c_re, c_im):
    g, p, gc = b_re.shape
    assert gc == S5_GC and p == S5_P
    dt = jnp.exp(log_dt.astype(F32))[:, None]
    mag = jnp.exp(lam_re * dt)
    ab_re = mag * jnp.cos(lam_im * dt)
    ab_im = mag * jnp.sin(lam_im * dt)
    den = lam_re * lam_re + lam_im * lam_im
    nr = ab_re - 1.0
    z_re = (nr * lam_re + ab_im * lam_im) / den
    z_im = (ab_im * lam_re - nr * lam_im) / den
    bb_re = z_re[..., None] * b_re - z_im[..., None] * b_im
    bb_im = z_re[..., None] * b_im + z_im[..., None] * b_re
    nkb = g // S5_GPB
    eye = jnp.eye(S5_GPB, dtype=F32)

    def in_blocks(bb):
        t = bb.reshape(nkb, S5_GPB, p, gc).transpose(0, 1, 3, 2)
        w = t[:, :, :, None, :] * eye[None, :, None, :, None]
        return w.reshape(nkb, S5_GPB * gc, S5_GPB * p)

    def out_blocks(cc):
        t = cc.reshape(nkb, S5_GPB, gc, p).transpose(0, 1, 3, 2)
        w = t[:, :, :, None, :] * eye[None, :, None, :, None]
        return w.reshape(nkb, S5_GPB * p, S5_GPB * gc)

    w_in = jnp.concatenate([in_blocks(bb_re), in_blocks(bb_im)], axis=2).astype(BF16)
    w_out = jnp.concatenate([out_blocks(c_re.astype(F32)), -out_blocks(c_im.astype(F32))], axis=1).astype(BF16)
    ncols = g * p // LANES
    return w_in, w_out, ab_re.reshape(ncols, LANES), ab_im.reshape(ncols, LANES)


def _s5_call(h, wts, d_skip, h0_re, h0_im, *, row_off, nseq, t_len, prev=None, name="s5"):
    rtot, d = h.shape
    w_in, w_out, a_re, a_im = wts
    ck = _pick(d, (1024, 512, 256))
    nkb = ck // LANES
    ncb = nkb * S5_CB
    assert ncb % SUBLANES == 0
    nkg = d // ck
    tl = _pick(t_len, (256, 128, 64, 32, 16, 8))
    pitch = tl + SUBLANES if (tl // SUBLANES) % 2 == 0 else tl + 2 * SUBLANES
    nt = t_len // tl
    assert row_off % tl == 0
    boff = row_off // tl
    kern = functools.partial(_s5_kernel, tl=tl, pitch=pitch, nkb=nkb)
    in_specs = [
        pl.BlockSpec((tl, ck), lambda kg, b, t: (boff + b * nt + t, kg)),
        pl.BlockSpec((nkb, LANES, 2 * S5_CB * LANES), lambda kg, b, t: (kg, 0, 0)),
        pl.BlockSpec((nkb, 2 * S5_CB * LANES, LANES), lambda kg, b, t: (kg, 0, 0)),
        pl.BlockSpec((ncb, LANES), lambda kg, b, t: (kg, 0)),
        pl.BlockSpec((ncb, LANES), lambda kg, b, t: (kg, 0)),
        pl.BlockSpec((1, ck), lambda kg, b, t: (0, kg)),
        pl.BlockSpec((None, ncb, LANES), lambda kg, b, t: (b, kg, 0)),
        pl.BlockSpec((None, ncb, LANES), lambda kg, b, t: (b, kg, 0)),
    ]
    args = [h, w_in, w_out, a_re, a_im, d_skip, h0_re, h0_im]
    n_in, aliases = _with_prev(args, in_specs, prev)
    nstate = a_re.shape[0]
    out_shape = [jax.ShapeDtypeStruct((rtot, d), BF16),
                 jax.ShapeDtypeStruct((nseq, nstate, LANES), F32),
                 jax.ShapeDtypeStruct((nseq, nstate, LANES), F32)]
    out_specs = [pl.BlockSpec((tl, ck), lambda kg, b, t: (boff + b * nt + t, kg)),
                 pl.BlockSpec((None, ncb, LANES), lambda kg, b, t: (b, kg, 0)),
                 pl.BlockSpec((None, ncb, LANES), lambda kg, b, t: (b, kg, 0))]
    return pl.pallas_call(
        _drop_aliased(kern, n_in, len(aliases)),
        out_shape=out_shape,
        grid=(nkg, nseq, nt),
        in_specs=in_specs,
        out_specs=out_specs,
        scratch_shapes=[pltpu.VMEM((ncb * pitch, LANES), F32), pltpu.VMEM((ncb * pitch, LANES), F32),
                        pltpu.VMEM((ncb, LANES), F32), pltpu.VMEM((ncb, LANES), F32)],
        input_output_aliases=aliases,
        compiler_params=_cparams(("arbitrary", "arbitrary", "arbitrary")),
        name=name,
    )(*args)


def _cumsum_kernel(x_ref, o_ref, carry, *, tl):
    t = pl.program_id(1)

    @pl.when(t == 0)
    def _():
        carry[...] = jnp.zeros_like(carry)

    x = x_ref[...]
    row = lax.broadcasted_iota(jnp.int32, (tl, tl), 0)
    col = lax.broadcasted_iota(jnp.int32, (tl, tl), 1)
    tri = (col <= row).astype(BF16)
    x_hi = x.astype(BF16)
    r1 = x - x_hi.astype(F32)
    x_mid = r1.astype(BF16)
    x_lo = (r1 - x_mid.astype(F32)).astype(BF16)
    c = (jnp.dot(tri, x_hi, preferred_element_type=F32)
         + jnp.dot(tri, x_mid, preferred_element_type=F32)
         + jnp.dot(tri, x_lo, preferred_element_type=F32)) + carry[...]
    o_ref[...] = c
    carry[...] = c[tl - 1:tl, :]


def _cumsum_time(x):
    b, t, hh = x.shape
    tl = _pick(t, (512, 256, 128)) if t % 128 == 0 else t
    return pl.pallas_call(
        functools.partial(_cumsum_kernel, tl=tl),
        out_shape=jax.ShapeDtypeStruct((b, t, hh), F32),
        grid=(b, t // tl),
        in_specs=[pl.BlockSpec((None, tl, hh), lambda i, j: (i, j, 0))],
        out_specs=pl.BlockSpec((None, tl, hh), lambda i, j: (i, j, 0)),
        scratch_shapes=[pltpu.VMEM((1, hh), F32)],
        compiler_params=_cparams(("arbitrary", "arbitrary")),
        name="cumsum_time",
    )(x)


def _flash_kernel(qt_ref, kt_ref, q_ref, k_ref, v_ref, ck_ref, o_ref, m_s, l_s, acc_s, *, tq, rc):
    p = pl.program_id(1)
    qi = qt_ref[p]
    kj = kt_ref[p]

    @pl.when(kj == 0)
    def _():
        m_s[...] = jnp.full_like(m_s, -jnp.inf)
        l_s[...] = jnp.zeros_like(l_s)
        acc_s[...] = jnp.zeros_like(acc_s)

    def update(rows, diagonal):
        s = (lax.dot_general(q_ref[rows, :], k_ref[...], (((1,), (1,)), ((), ())), preferred_element_type=F32)
             - ck_ref[...] * LOG2E)
        if diagonal:
            row = lax.broadcasted_iota(jnp.int32, s.shape, 0) + rows.start
            col = lax.broadcasted_iota(jnp.int32, s.shape, 1)
            s = jnp.where(col <= row, s, -jnp.inf)
        m_prev = m_s[rows, :]
        m_new = jnp.maximum(m_prev, jnp.max(s, axis=1, keepdims=True))
        a = jnp.exp2(m_prev - m_new)
        pexp = jnp.exp2(s - m_new)
        vv = jnp.concatenate([v_ref[...], jnp.ones_like(v_ref[...])], axis=1)
        acc_s[rows, :] = a * acc_s[rows, :] + jnp.dot(pexp.astype(BF16), vv, preferred_element_type=F32)
        m_s[rows, :] = m_new

    chunks = [slice(c, c + rc) for c in range(0, tq, rc)]

    @pl.when(kj < qi)
    def _():
        for rows in chunks:
            update(rows, False)

    @pl.when(kj == qi)
    def _():
        for rows in chunks:
            update(rows, True)
        acc = acc_s[...]
        hd = acc.shape[1] // 2
        o_ref[...] = (acc[:, :hd] / acc[:, hd:hd + 1]).astype(o_ref.dtype)


def _flash_prefill(q, k, v, cum_t, *, t_len):
    heads, rtot, hd = q.shape
    d = heads * hd
    tq = _pick(t_len, (1024, 512, 256, 128))
    nq = t_len // tq
    pairs = [(i, j) for i in range(nq) for j in range(i + 1)]
    qt = jnp.asarray([a for a, _ in pairs], jnp.int32)
    kt = jnp.asarray([b for _, b in pairs], jnp.int32)
    grid_spec = pltpu.PrefetchScalarGridSpec(
        num_scalar_prefetch=2,
        grid=(heads, len(pairs)),
        in_specs=[pl.BlockSpec((None, tq, hd), lambda h, p, qt, kt: (h, qt[p], 0)),
                  pl.BlockSpec((None, tq, hd), lambda h, p, qt, kt: (h, kt[p], 0)),
                  pl.BlockSpec((None, tq, hd), lambda h, p, qt, kt: (h, kt[p], 0)),
                  pl.BlockSpec((None, 1, tq), lambda h, p, qt, kt: (h, 0, kt[p]))],
        out_specs=pl.BlockSpec((tq, hd), lambda h, p, qt, kt: (qt[p], h)),
        scratch_shapes=[pltpu.VMEM((tq, 1), F32), pltpu.VMEM((tq, 1), F32), pltpu.VMEM((tq, 2 * hd), F32)],
    )
    return pl.pallas_call(
        functools.partial(_flash_kernel, tq=tq, rc=_pick(tq, (256, 128))),
        out_shape=jax.ShapeDtypeStruct((rtot, d), BF16),
        grid_spec=grid_spec,
        compiler_params=_cparams(("arbitrary", "arbitrary")),
        name="fox_prefill",
    )(qt, kt, q, k, v, cum_t)


def _decode_kernel(q_ref, kn_ref, vn_ref, kc_ref, vc_ref, ck_ref, o_ref, kbuf, vbuf, sem, *,
                   past, s_len, hpb, hd, heads, layer):
    b = pl.program_id(0)
    g = pl.program_id(1)
    ng = pl.num_programs(1)
    step = b * ng + g
    slot = lax.rem(step, 2)

    def head_copies(bb, gg, s):
        out = []
        for hh in range(hpb):
            head = gg * hpb + hh
            out.append(pltpu.make_async_copy(kc_ref.at[layer, bb, :, head, :], kbuf.at[s, hh], sem.at[s]))
            out.append(pltpu.make_async_copy(vc_ref.at[layer, bb, :, head, :], vbuf.at[s, hh], sem.at[s]))
        return out

    @pl.when(step == 0)
    def _():
        for c in head_copies(0, 0, 0):
            c.start()

    for c in head_copies(b, g, slot):
        c.wait()

    @pl.when(step + 1 < pl.num_programs(0) * ng)
    def _():
        nxt = step + 1
        nb_ = nxt // ng
        for c in head_copies(nb_, nxt - nb_ * ng, 1 - slot):
            c.start()

    nt = (((1,), (1,)), ((), ()))
    row = lax.broadcasted_iota(jnp.int32, (s_len, s_len), 0)
    col = lax.broadcasted_iota(jnp.int32, (s_len, s_len), 1)
    for hh in range(hpb):
        cols = slice(hh * hd, (hh + 1) * hd)
        q = q_ref[hh]
        ck = ck_ref[hh] * LOG2E
        kc = kbuf[slot, hh].astype(BF16)
        vc = vbuf[slot, hh].astype(BF16)
        s_c = lax.dot_general(q, kc, nt, preferred_element_type=F32) - ck[:, :past]
        s_n = lax.dot_general(q, kn_ref[hh], nt, preferred_element_type=F32) - ck[:, past:]
        s_n = jnp.where(col <= row, s_n, -jnp.inf)
        m = jnp.maximum(jnp.max(s_c, axis=1, keepdims=True), jnp.max(s_n, axis=1, keepdims=True))
        p_c = jnp.exp2(s_c - m)
        p_n = jnp.exp2(s_n - m)
        l = jnp.sum(p_c, axis=1, keepdims=True) + jnp.sum(p_n, axis=1, keepdims=True)
        acc = (jnp.dot(p_c.astype(BF16), vc, preferred_element_type=F32)
               + jnp.dot(p_n.astype(BF16), vn_ref[hh], preferred_element_type=F32))
        o_ref[:, cols] = (acc / l).astype(o_ref.dtype)


def _fox_decode(q, k, v, k_cache, v_cache, cum_t, *, row_off, nseq, s_len, layer, prev):
    heads, rtot, hd = q.shape
    d = heads * hd
    nlyr, nbatch, past = k_cache.shape[:3]
    assert row_off % s_len == 0 and past % LANES == 0
    boff = row_off // s_len
    hpb = _pick(heads, (4, 2, 1))
    new_spec = pl.BlockSpec((hpb, s_len, hd), lambda b, h: (h, boff + b, 0))
    out_spec = pl.BlockSpec((s_len, hpb * hd), lambda b, h: (boff + b, h))
    any_spec = pl.BlockSpec(memory_space=pl.ANY)
    in_specs = [new_spec, new_spec, new_spec, any_spec, any_spec,
                pl.BlockSpec((None, hpb, 1, past + s_len), lambda b, h: (b, h, 0, 0))]
    args = [q, k, v, k_cache, v_cache, cum_t]
    n_in, aliases = _with_prev(args, in_specs, prev)
    kern = functools.partial(_decode_kernel, past=past, s_len=s_len, hpb=hpb, hd=hd, heads=heads, layer=layer)
    return pl.pallas_call(
        _drop_aliased(kern, n_in, len(aliases)),
        out_shape=[jax.ShapeDtypeStruct((rtot, d), BF16)],
        grid=(nseq, heads // hpb),
        in_specs=in_specs,
        out_specs=[out_spec],
        scratch_shapes=[pltpu.VMEM((2, hpb, past, hd), F32), pltpu.VMEM((2, hpb, past, hd), F32),
                        pltpu.SemaphoreType.DMA((2,))],
        input_output_aliases=aliases,
        compiler_params=_cparams(("arbitrary", "arbitrary")),
        name="fox_decode",
    )(*args)[0]


def _conv_kernel(gb_ref, gc_ref, hin_ref, cw_ref, buf_ref, o_ref, zl_ref, carry, *, tl):
    t = pl.program_id(2)

    @pl.when(t == 0)
    def _():
        carry[...] = buf_ref[...]

    z = gc_ref[...] * hin_ref[...]
    c0 = carry[0:1, :]
    c1 = carry[1:2, :]
    row = lax.broadcasted_iota(jnp.int32, z.shape, 0)
    zm1 = jnp.where(row == 0, c1, pltpu.roll(z, 1, axis=0))
    zm2 = jnp.where(row == 0, c0, jnp.where(row == 1, c1, pltpu.roll(z, 2, axis=0)))
    cw = cw_ref[...]
    acc = cw[0:1, :] * zm2 + cw[1:2, :] * zm1 + cw[2:3, :] * z
    o_ref[...] = (gb_ref[...] * acc).astype(o_ref.dtype)
    last = z[tl - 2:tl, :]
    carry[...] = last
    zl_ref[...] = last


def _conv_call(proj, conv_w, layer, buf, *, row_off, nseq, t_len, d, prev=None, name="conv"):
    rtot = proj.shape[0]
    assert conv_w.shape[1] == 3 and t_len >= 2
    tn = _pick(d, (1024, 512, 256, 128))
    tl = _pick(t_len, (256, 128, 64, 32, 16, 8))
    nt = t_len // tl
    nd = d // tn
    assert row_off % tl == 0
    boff = row_off // tl

    def pspec(part):
        return pl.BlockSpec((tl, tn), lambda b, j, t: (boff + b * nt + t, part * nd + j))

    in_specs = [pspec(0), pspec(1), pspec(2),
                pl.BlockSpec((None, 3, tn), lambda b, j, t: (layer, 0, j)),
                pl.BlockSpec((None, 2, tn), lambda b, j, t: (b, 0, j))]
    args = [proj, proj, proj, conv_w, buf]
    n_in, aliases = _with_prev(args, in_specs, prev)
    return pl.pallas_call(
        _drop_aliased(functools.partial(_conv_kernel, tl=tl), n_in, len(aliases)),
        out_shape=[jax.ShapeDtypeStruct((rtot, d), BF16), jax.ShapeDtypeStruct((nseq, 2, d), F32)],
        grid=(nseq, nd, nt),
        in_specs=in_specs,
        out_specs=[pl.BlockSpec((tl, tn), lambda b, j, t: (boff + b * nt + t, j)),
                   pl.BlockSpec((None, 2, tn), lambda b, j, t: (b, 0, j))],
        scratch_shapes=[pltpu.VMEM((2, tn), F32)],
        input_output_aliases=aliases,
        compiler_params=_cparams(("arbitrary", "arbitrary", "arbitrary")),
        name=name,
    )(*args)


WEIGHT_LOOKAHEAD = 2
WEIGHT_RING = WEIGHT_LOOKAHEAD + 1


def _expert_kernel(be_ref, nu_ref, ss_ref, nr_ref, tok_ref, h_ref, wg_ref, wu_ref, wd_ref, o_ref,
                   xbuf, sem, gu_buf, gu_sem, d_buf, d_sem, hg, hu, hb, *, layer, nkt, nnt, tk, tn):
    b = pl.program_id(0)
    j = pl.program_id(1)
    nu = nu_ref[0]
    slot = lax.rem(b, 2)
    nsteps = nkt + nnt

    def row_copy(s, r, tok):
        return pltpu.make_async_copy(h_ref.at[pl.ds(tok, 1)], xbuf.at[s, pl.ds(r, 1)], sem.at[s])

    def start_block(bb, s):
        base = ss_ref[bb]

        def body(r, c):
            row_copy(s, r, tok_ref[base + r]).start()
            return c
        lax.fori_loop(0, nr_ref[bb], body, 0)

    def wait_block(bb, s):
        n = nr_ref[bb]

        def body8(r, c):
            pltpu.make_async_copy(h_ref.at[pl.ds(0, SUBLANES)], xbuf.at[s, pl.ds(0, SUBLANES)], sem.at[s]).wait()
            return c

        def body1(r, c):
            row_copy(s, 0, 0).wait()
            return c
        lax.fori_loop(0, lax.shift_right_logical(n, 3), body8, 0)
        lax.fori_loop(0, lax.bitwise_and(n, SUBLANES - 1), body1, 0)

    def gu_copies(e, jt, ring_slot):
        k0 = pl.multiple_of(jt * tk, tk)
        return (pltpu.make_async_copy(wg_ref.at[layer, e, pl.ds(k0, tk), :], gu_buf.at[ring_slot, 0],
                                      gu_sem.at[ring_slot]),
                pltpu.make_async_copy(wu_ref.at[layer, e, pl.ds(k0, tk), :], gu_buf.at[ring_slot, 1],
                                      gu_sem.at[ring_slot]))

    def d_copy(e, nt, ring_slot):
        n0 = pl.multiple_of(nt * tn, tn)
        return pltpu.make_async_copy(wd_ref.at[layer, e, :, pl.ds(n0, tn)], d_buf.at[ring_slot],
                                     d_sem.at[ring_slot])

    def start_tiles(step):
        b2 = step // nsteps
        j2 = step - b2 * nsteps

        @pl.when(b2 < nu)
        def _():
            e2 = be_ref[b2]

            @pl.when(j2 < nkt)
            def _():
                for c in gu_copies(e2, j2, lax.rem(b2 * nkt + j2, WEIGHT_RING)):
                    c.start(priority=1)

            @pl.when(j2 >= nkt)
            def _():
                d_copy(e2, j2 - nkt, lax.rem(b2 * nnt + j2 - nkt, WEIGHT_RING)).start(priority=1)

    step = b * nsteps + j

    @pl.when(step == 0)
    def _():
        xbuf[...] = jnp.zeros_like(xbuf)
        start_block(0, 0)
        for ahead in range(WEIGHT_LOOKAHEAD):
            start_tiles(ahead)

    @pl.when(b < nu)
    def _():
        start_tiles(step + WEIGHT_LOOKAHEAD)

    @pl.when((b < nu) & (j == 0))
    def _():
        wait_block(b, slot)

        @pl.when(b + 1 < nu)
        def _():
            start_block(b + 1, 1 - slot)

    @pl.when(b < nu)
    def _():
        for jj in range(nkt):
            @pl.when(j == jj)
            def _(jj=jj):
                ring_slot = lax.rem(b * nkt + jj, WEIGHT_RING)
                for c in gu_copies(0, 0, ring_slot):
                    c.wait()
                xb = xbuf[slot, :, jj * tk:(jj + 1) * tk].astype(BF16)
                pg = jnp.dot(xb, gu_buf[ring_slot, 0].astype(BF16), preferred_element_type=F32)
                pu = jnp.dot(xb, gu_buf[ring_slot, 1].astype(BF16), preferred_element_type=F32)
                if jj > 0:
                    pg = hg[...] + pg
                    pu = hu[...] + pu
                if jj < nkt - 1:
                    hg[...] = pg
                    hu[...] = pu
                else:
                    hb[...] = (pg * jax.nn.sigmoid(pg) * pu).astype(BF16)

        @pl.when(j >= nkt)
        def _():
            ring_slot = lax.rem(b * nnt + j - nkt, WEIGHT_RING)
            d_copy(0, 0, ring_slot).wait()
            o_ref[...] = jnp.dot(hb[...], d_buf[ring_slot].astype(BF16), preferred_element_type=F32)


def _expert_call(h, plan, w_gate, w_up, w_down, layer, *, tm):
    block_expert, n_used, src_start, nrows, sorted_tok = plan
    d = h.shape[1]
    de = w_gate.shape[-1]
    nb = block_expert.shape[0]
    rows = nb * tm
    tk = _pick(d, (1024, 512, 256, 128))
    tn = tk
    nkt = d // tk
    nnt = d // tn
    last_n = nnt - 1

    def blk(b, nu):
        return jnp.minimum(b, nu[0] - 1)

    def nidx(b, j, nu):
        return jnp.where(b < nu[0], jnp.maximum(j - nkt, 0), last_n)

    any_spec = pl.BlockSpec(memory_space=pl.ANY)
    grid_spec = pltpu.PrefetchScalarGridSpec(
        num_scalar_prefetch=5,
        grid=(nb, nkt + nnt),
        in_specs=[any_spec, any_spec, any_spec, any_spec],
        out_specs=pl.BlockSpec((tm, tn), lambda b, j, be, nu, ss, nr, tok: (blk(b, nu), nidx(b, j, nu))),
        scratch_shapes=[pltpu.VMEM((2, tm, d), F32), pltpu.SemaphoreType.DMA((2,)),
                        pltpu.VMEM((WEIGHT_RING, 2, tk, de), F32), pltpu.SemaphoreType.DMA((WEIGHT_RING,)),
                        pltpu.VMEM((WEIGHT_RING, de, tn), F32), pltpu.SemaphoreType.DMA((WEIGHT_RING,)),
                        pltpu.VMEM((tm, de), F32), pltpu.VMEM((tm, de), F32), pltpu.VMEM((tm, de), BF16)],
    )
    return pl.pallas_call(
        functools.partial(_expert_kernel, layer=layer, nkt=nkt, nnt=nnt, tk=tk, tn=tn),
        out_shape=jax.ShapeDtypeStruct((rows, d), F32),
        grid_spec=grid_spec,
        compiler_params=_cparams(("arbitrary", "arbitrary")),
        name="moe_experts",
    )(block_expert, n_used, src_start, nrows, sorted_tok, h, w_gate, w_up, w_down)


def _dispatch(ids, n_experts, tm):
    r, k = ids.shape
    a = r * k
    flat_e = ids.reshape(a)
    order = jnp.argsort(flat_e, stable=True).astype(jnp.int32)
    inv = jnp.argsort(order).astype(jnp.int32)
    onehot = flat_e[:, None] == jnp.arange(n_experts, dtype=jnp.int32)[None, :]
    counts = jnp.sum(onehot, axis=0, dtype=jnp.int32)
    padded = (counts + tm - 1) // tm * tm
    pad_end = jnp.cumsum(padded)
    pad_start = pad_end - padded
    start = jnp.cumsum(counts) - counts
    shift = jnp.sum(jnp.where(onehot, (pad_start - start)[None, :], 0), axis=1, dtype=jnp.int32)
    slot_of_assign = (inv + shift).reshape(r, k)
    nb = -(-a // tm) + n_experts
    blk_row = jnp.arange(nb, dtype=jnp.int32) * tm
    block_expert = jnp.minimum(jnp.sum(pad_end[None, :] <= blk_row[:, None], axis=1, dtype=jnp.int32),
                               n_experts - 1)
    q = blk_row - pad_start[block_expert]
    nrows = jnp.clip(counts[block_expert] - q, 0, tm).astype(jnp.int32)
    src_start = jnp.clip(start[block_expert] + q, 0, a - 1).astype(jnp.int32)
    n_used = (pad_end[-1] // tm).astype(jnp.int32).reshape(1)
    sorted_tok = order // k
    return (block_expert, n_used, src_start, nrows, sorted_tok), slot_of_assign


def _moe(h, route, w_gate, w_up, w_down, layer):
    n_experts = w_gate.shape[1]
    ids = route[:, 0:2].astype(jnp.int32)
    a = ids.size
    tm = _pick(a, (512, 256, 128, 64, 32, 16, 8))
    plan, slot_of_assign = _dispatch(ids, n_experts, tm)
    ys = _expert_call(h, plan, w_gate, w_up, w_down, layer, tm=tm)
    return ys, slot_of_assign.T.reshape(a)


def kernel(x_prompt, x_sample, c_prompt, c_sample, state_ssm_re, state_ssm_im, cache_fox_k, cache_fox_v, cache_fox_logf, state_conv, ssm_lam_re, ssm_lam_im, ssm_log_dt, ssm_b_re, ssm_b_im, ssm_c_re, ssm_c_im, ssm_d, ssm_w_glu, ssm_b_glu, fox_w_qkvf, fox_b_f, fox_w_o, conv_w_in, conv_w, conv_w_out, ada_w, ada_b, ln_g, ln_b, moe_w_group, moe_b_group, moe_w_expert, moe_b_expert, moe_w_gate, moe_w_up, moe_w_down):
    bp, seq, d = x_prompt.shape
    nbs, dseq, _ = x_sample.shape
    depth = ada_w.shape[0]
    heads, hd = cache_fox_k.shape[3], cache_fox_k.shape[4]
    past = cache_fox_k.shape[2]
    n_groups = moe_w_group.shape[-1]
    n_experts = moe_w_expert.shape[-1]
    assert bp == 1 and n_groups == 8 and n_experts == 64 and heads * hd == d
    alpha = float((2 * depth) ** 0.25)
    rp, rs = bp * seq, nbs * dseq
    geom = dict(rp=rp, rs=rs, seq=seq, dseq=dseq, nbs=nbs)

    x = (x_prompt.reshape(rp, d), x_sample.reshape(rs, d))

    nmod = -(-(nbs + bp) // SUBLANES) * SUBLANES
    c_all = jnp.concatenate([c_sample, c_prompt, jnp.zeros((nmod - nbs - bp, d), F32)], axis=0)
    mods = jnp.stack([_matmul(c_all, ada_w, i, n_out=6 * d, bias=ada_b, lhs_silu=True, name="ada")[0]
                      for i in range(depth)]).reshape(depth, nmod, 1, 6 * d)
    ln_g4 = ln_g.reshape(depth, 2, 1, d)
    ln_b4 = ln_b.reshape(depth, 2, 1, d)
    SH1, SC1, G1, SH2, SC2, G2 = range(6)

    def mixer_h_dtype(i):
        return F32 if i % N_MIXERS == 0 else BF16

    def router_weights(i):
        w = jnp.concatenate([moe_w_group[i], moe_w_expert[i],
                             jnp.zeros((d, LANES - n_groups - n_experts), F32)], axis=1)
        b = jnp.concatenate([moe_b_group[i], moe_b_expert[i],
                             jnp.zeros((LANES - n_groups - n_experts,), F32)]).reshape(1, LANES)
        w_hi = w.astype(BF16)
        w_lo = (w - w_hi.astype(F32)).astype(BF16)
        return w_hi, w_lo, b

    (h,) = _ln_both(x, mods, geom, h_mods=(0, SC1, SH1), h_dtype=mixer_h_dtype(0))

    re_p, im_p, re_s, im_s = [], [], [], []
    k_p, v_p, lf_p, k_s, v_s, lf_s = [], [], [], [], [], []
    conv_p, conv_s = [], []
    for i in range(depth):
        j = i // N_MIXERS
        kind = i % N_MIXERS
        if kind == 0:
            wts = _s5_weights(ssm_lam_re[j].astype(F32), ssm_lam_im[j].astype(F32), ssm_log_dt[j],
                              ssm_b_re[j].astype(F32), ssm_b_im[j].astype(F32), ssm_c_re[j], ssm_c_im[j])
            nstate = wts[2].shape[0]
            zero_state = jnp.zeros((bp, nstate, LANES), F32)
            d_skip = ssm_d[j].reshape(1, d)
            outs = _s5_call(h, wts, d_skip, zero_state, zero_state, row_off=0, nseq=bp, t_len=seq,
                            name="s5_prompt")
            outs2 = _s5_call(h, wts, d_skip, state_ssm_re[j].reshape(nbs, nstate, LANES),
                             state_ssm_im[j].reshape(nbs, nstate, LANES), row_off=rp, nseq=nbs,
                             t_len=dseq, prev=[outs[0]], name="s5_sample")
            gshape = ssm_lam_re.shape[1:]
            re_p.append(outs[1].reshape((bp,) + gshape))
            im_p.append(outs[2].reshape((bp,) + gshape))
            re_s.append(outs2[1].reshape((nbs,) + gshape))
            im_s.append(outs2[2].reshape((nbs,) + gshape))
            (out,) = _matmul(outs2[0], _weight_bf16(ssm_w_glu, j, 2 * d), 0, n_out=d, bias=ssm_b_glu[j:j + 1],
                             glu=True, name="s5_glu")
        elif kind == 1:
            w_qkv = _weight_bf16(fox_w_qkvf, j, 3 * d)
            (qh,) = _matmul(h, w_qkv, 0, n_out=d, col_off=0, row_out=False, heads_hd=hd,
                            heads_scale=float(hd) ** -0.5 * LOG2E, name="fox_q")
            k, kh = _matmul(h, w_qkv, 0, n_out=d, col_off=d, heads_hd=hd, name="fox_k")
            v, vh = _matmul(h, w_qkv, 0, n_out=d, col_off=2 * d, heads_hd=hd, name="fox_v")
            w_f = fox_w_qkvf[:, :, 3 * d:]
            (lf,) = _matmul(h, w_f, j, n_out=heads, bias=fox_b_f, epilogue="logsigmoid", name="fox_f")
            lf_prompt = lf[:rp].reshape(bp, seq, heads)
            lf_sample = lf[rp:].reshape(nbs, dseq, heads)
            cum_p = _cumsum_time(lf_prompt)
            cum_s = _cumsum_time(jnp.concatenate([cache_fox_logf[j].astype(F32), lf_sample], axis=1))
            o_p = _flash_prefill(qh, kh, vh, cum_p[0].T.reshape(heads, 1, seq), t_len=seq)
            o = _fox_decode(qh, kh, vh, cache_fox_k, cache_fox_v,
                            cum_s.transpose(0, 2, 1).reshape(nbs, heads, 1, past + dseq),
                            row_off=rp, nseq=nbs, s_len=dseq, layer=j, prev=[o_p])
            (out,) = _matmul(o, _weight_bf16(fox_w_o, j, d), 0, n_out=d, name="fox_o")
            k_p.append(k[:rp].reshape(bp, seq, heads, hd))
            v_p.append(v[:rp].reshape(bp, seq, heads, hd))
            lf_p.append(lf_prompt)
            k_s.append(k[rp:].reshape(nbs, dseq, heads, hd))
            v_s.append(v[rp:].reshape(nbs, dseq, heads, hd))
            lf_s.append(lf_sample)
        else:
            (proj,) = _matmul(h, _weight_bf16(conv_w_in, j, 3 * d), 0, n_out=3 * d, name="conv_in")
            zero_buf = jnp.zeros((bp, conv_w.shape[1] - 1, d), F32)
            outs = _conv_call(proj, conv_w, j, zero_buf, row_off=0, nseq=bp, t_len=seq, d=d, name="conv_prompt")
            outs2 = _conv_call(proj, conv_w, j, state_conv[j].astype(F32), row_off=rp, nseq=nbs, t_len=dseq,
                               d=d, prev=[outs[0]], name="conv_sample")
            conv_p.append(outs[1])
            conv_s.append(outs2[1])
            (out,) = _matmul(outs2[0], _weight_bf16(conv_w_out, j, d), 0, n_out=d, name="conv_out")

        x, h2, route = _ln_both(x, mods, geom, alpha=alpha, o=out, res_mod=(i, G1), ln_g=ln_g4, ln_b=ln_b4,
                                ln_idx=(i, 0), h_mods=(i, SC2, SH2), h_dtype=F32,
                                router_w=router_weights(i))
        ys, slots = _moe(h2, route, moe_w_gate, moe_w_up, moe_w_down, i)
        ffn = dict(alpha=alpha, o=ys, slots=slots, route_in=route, res_mod=(i, G2),
                   ln_g=ln_g4, ln_b=ln_b4, ln_idx=(i, 1))
        if i + 1 < depth:
            x, h = _ln_both(x, mods, geom, h_mods=(i + 1, SC1, SH1), h_dtype=mixer_h_dtype(i + 1), **ffn)
        else:
            (x,) = _ln_both(x, mods, geom, **ffn)

    y_prompt = x[:rp].reshape(bp, seq, d)
    y_sample = x[rp:].reshape(nbs, dseq, d)
    return (y_prompt, y_sample, jnp.stack(re_p), jnp.stack(im_p), jnp.stack(k_p), jnp.stack(v_p),
            jnp.stack(lf_p), jnp.stack(conv_p), jnp.stack(re_s), jnp.stack(im_s), jnp.stack(k_s),
            jnp.stack(v_s), jnp.stack(lf_s), jnp.stack(conv_s))
```
